```python
import math
import jax, jax.numpy as jnp
from jax import lax
import numpy as np

D_MODEL = 2048
BATCH = 4
SEQ = 2048
DEPTH = 2
DEC_BATCH = 128
DEC_SEQ = 8
PAST_LEN = 8192
PAGE_SIZE = 128

N_BRANCH = 4
SSM_INNER = D_MODEL // 2
SSM_HEADDIM = 64
SSM_HEADS = SSM_INNER // SSM_HEADDIM
SSM_GROUPS = 4
SSM_STATE = 128
SSM_CONV = 4
SSM_CHUNK = 128
SSM_CONV_DIM = SSM_INNER + 2 * SSM_GROUPS * SSM_STATE
CF_WIDTH = D_MODEL // 2
CF_CONV_WIDTH = 31
SC_WIDTH = D_MODEL // 2
SC_CONV_WIDTH = 3
ATT_HEAD_DIM = 64
ATT_HEADS = 16
ATT_KV_HEADS = 4
ATT_REP = ATT_HEADS // ATT_KV_HEADS
ATT_WIDTH = ATT_HEADS * ATT_HEAD_DIM
ATT_KV_WIDTH = ATT_KV_HEADS * ATT_HEAD_DIM
WINDOW = 128
ATT_BLOCK = 128
WIN_BUF = min(WINDOW, PAST_LEN)
EPS = 1e-6

IN_SPLITS = (SSM_INNER, SSM_CONV_DIM, SSM_HEADS,
             2 * CF_WIDTH, CF_WIDTH,
             3 * SC_WIDTH, SC_WIDTH,
             ATT_WIDTH, ATT_KV_WIDTH, ATT_KV_WIDTH, ATT_WIDTH,
             N_BRANCH * D_MODEL)
IN_COLS = sum(IN_SPLITS)

kernel_name = "hybrid_ssd_conformer_shortconv_swa_step"


def _in_offsets():
    return [int(o) for o in np.cumsum(IN_SPLITS)[:-1]]


def rms_norm(x, w, eps=EPS):
    xf = x.astype(jnp.float32)
    xf = xf * lax.rsqrt(jnp.mean(xf * xf, axis=-1, keepdims=True) + eps)
    return (xf * w.astype(jnp.float32)).astype(x.dtype)


def layer_norm(x, w, b, eps=1e-5):
    xf = x.astype(jnp.float32)
    mu = jnp.mean(xf, axis=-1, keepdims=True)
    xc = xf - mu
    var = jnp.mean(xc * xc, axis=-1, keepdims=True)
    return (xc * lax.rsqrt(var + eps) * w.astype(jnp.float32) + b.astype(jnp.float32)).astype(x.dtype)


def causal_dwconv(x, prefix, w, b=None):
    k = w.shape[0]
    xp = jnp.concatenate([prefix.astype(x.dtype), x], axis=1)
    y = lax.conv_general_dilated(xp, w[:, None, :].astype(x.dtype), window_strides=(1,),
                                 padding='VALID', dimension_numbers=('NWC', 'WIO', 'NWC'),
                                 feature_group_count=x.shape[-1])
    if b is not None:
        y = y + b.astype(x.dtype)
    return y, xp[:, xp.shape[1] - (k - 1):]


def segsum_exp(cs):
    n = cs.shape[-1]
    diff = cs[..., :, None] - cs[..., None, :]
    mask = jnp.tril(jnp.ones((n, n), dtype=bool))
    return jnp.exp(jnp.where(mask, diff, -jnp.inf))


def ssd_scan(x, dt, a, bmat, cmat, h0):
    b_, t_, h_, p_ = x.shape
    g_, n_ = bmat.shape[2], bmat.shape[3]
    r_ = h_ // g_
    lc = SSM_CHUNK if t_ % SSM_CHUNK == 0 else t_
    nc = t_ // lc
    xf = (x.astype(jnp.float32) * dt[..., None]).reshape(b_, nc, lc, g_, r_, p_)
    bf = bmat.astype(jnp.float32).reshape(b_, nc, lc, g_, n_)
    cf = cmat.astype(jnp.float32).reshape(b_, nc, lc, g_, n_)
    da = (dt * a).reshape(b_, nc, lc, g_, r_)
    cs = jnp.cumsum(da, axis=2)
    lmat = segsum_exp(jnp.moveaxis(cs, 2, -1))
    cb = jnp.einsum('bclgn,bcsgn->bcgls', cf, bf)
    y_diag = jnp.einsum('bcgrls,bcsgrp->bclgrp', cb[:, :, :, None] * lmat, xf)
    decay = jnp.exp(cs[:, :, -1:] - cs)
    chunk_states = jnp.einsum('bclgn,bclgr,bclgrp->bcgrpn', bf, decay, xf)
    chunk_decay = jnp.exp(cs[:, :, -1])

    def step(h, inp):
        s_c, d_c = inp
        return h * d_c[..., None, None] + s_c, h

    h0g = h0.astype(jnp.float32).reshape(b_, g_, r_, p_, n_)
    h_final, h_prev = lax.scan(step, h0g, (jnp.moveaxis(chunk_states, 1, 0),
                                           jnp.moveaxis(chunk_decay, 1, 0)))
    h_prev = jnp.moveaxis(h_prev, 0, 1)
    y_off = jnp.einsum('bclgn,bcgrpn,bclgr->bclgrp', cf, h_prev, jnp.exp(cs))
    y = (y_diag + y_off).reshape(b_, t_, h_, p_)
    return y, h_final.reshape(b_, h_, p_, n_)


def swa_sink_attend(q, k, v, q_pos, k_pos, sinks):
    s = jnp.einsum('bnqhrd,bnkhd->bnhrqk', q.astype(jnp.float32), k.astype(jnp.float32))
    s = s * (ATT_HEAD_DIM ** -0.5)
    rel = q_pos[:, :, None] - k_pos[:, None, :]
    valid = (rel >= 0) & (rel < WINDOW) & (k_pos[:, None, :] >= 0)
    s = jnp.where(valid[None, :, None, None], s, -jnp.inf)
    sink = sinks.astype(jnp.float32).reshape(1, 1, ATT_KV_HEADS, ATT_REP, 1, 1)
    m = jnp.maximum(jnp.max(s, axis=-1, keepdims=True), sink)
    p = jnp.exp(s - m)
    denom = jnp.sum(p, axis=-1, keepdims=True) + jnp.exp(sink - m)
    return jnp.einsum('bnhrqk,bnkhd->bnqhrd', p / denom, v.astype(jnp.float32))


def decoder_layer(x, h_ssm, buf_ssm, buf_cf, buf_sc, kv_buf, lp, prompt):
    (norm_w, w_in, ssm_conv_w, ssm_conv_b, dt_bias, a_log, d_skip, ssm_norm_w, w_out_ssm,
     cf_conv_w, cf_conv_b, cf_ln_w, cf_ln_b, w_out_cf, sc_conv_w, w_out_sc,
     sinks, w_out_att, w_o) = lp
    b_, t_, _ = x.shape
    xn = rms_norm(x, norm_w)
    proj = jnp.einsum('btd,de->bte', xn, w_in.astype(x.dtype))
    (z, xbc, dt_raw, cf_in, cf_gate, sc_in, sc_gate,
     q, k, v, att_gate, merge) = jnp.split(proj, _in_offsets(), axis=-1)

    xbc, new_buf_ssm = causal_dwconv(xbc, buf_ssm, ssm_conv_w, ssm_conv_b)
    xbc = jax.nn.silu(xbc)
    xs, bs, cs = jnp.split(xbc, [SSM_INNER, SSM_INNER + SSM_GROUPS * SSM_STATE], axis=-1)
    dt = jax.nn.softplus(dt_raw.astype(jnp.float32) + dt_bias.astype(jnp.float32))
    a = -jnp.exp(a_log.astype(jnp.float32))
    xh = xs.reshape(b_, t_, SSM_HEADS, SSM_HEADDIM)
    y_a, new_h = ssd_scan(xh, dt, a, bs.reshape(b_, t_, SSM_GROUPS, SSM_STATE),
                          cs.reshape(b_, t_, SSM_GROUPS, SSM_STATE), h_ssm)
    y_a = y_a + d_skip.astype(jnp.float32)[:, None] * xh.astype(jnp.float32)
    y_a = y_a.reshape(b_, t_, SSM_INNER).astype(x.dtype) * jax.nn.silu(z)
    out_a = rms_norm(y_a, ssm_norm_w, 1e-5) @ w_out_ssm.astype(x.dtype)

    u = cf_in[..., :CF_WIDTH] * jax.nn.sigmoid(cf_in[..., CF_WIDTH:])
    u, new_buf_cf = causal_dwconv(u, buf_cf, cf_conv_w, cf_conv_b)
    u = jax.nn.silu(layer_norm(u, cf_ln_w, cf_ln_b))
    out_b = (u * jax.nn.silu(cf_gate)) @ w_out_cf.astype(x.dtype)

    gb, gc, sv = jnp.split(sc_in, 3, axis=-1)
    u, new_buf_sc = causal_dwconv(gc * sv, buf_sc, sc_conv_w)
    out_c = (gb * u * jax.nn.silu(sc_gate)) @ w_out_sc.astype(x.dtype)

    q = q.reshape(b_, t_, ATT_KV_HEADS, ATT_REP, ATT_HEAD_DIM)
    k = k.reshape(b_, t_, ATT_KV_HEADS, ATT_HEAD_DIM)
    v = v.reshape(b_, t_, ATT_KV_HEADS, ATT_HEAD_DIM)
    if prompt:
        nb = t_ // ATT_BLOCK
        qb = q.reshape(b_, nb, ATT_BLOCK, ATT_KV_HEADS, ATT_REP, ATT_HEAD_DIM)
        kb = k.reshape(b_, nb, ATT_BLOCK, ATT_KV_HEADS, ATT_HEAD_DIM)
        vb = v.reshape(b_, nb, ATT_BLOCK, ATT_KV_HEADS, ATT_HEAD_DIM)
        k_band = jnp.concatenate([jnp.concatenate([jnp.zeros_like(kb[:, :1]), kb[:, :-1]], axis=1), kb], axis=2)
        v_band = jnp.concatenate([jnp.concatenate([jnp.zeros_like(vb[:, :1]), vb[:, :-1]], axis=1), vb], axis=2)
        pos = jnp.arange(t_, dtype=jnp.int32).reshape(nb, ATT_BLOCK)
        k_pos = jnp.concatenate([pos - ATT_BLOCK, pos], axis=1)
        o = swa_sink_attend(qb, k_band, v_band, pos, k_pos, sinks)
        new_k = k[:, t_ - WIN_BUF:]
        new_v = v[:, t_ - WIN_BUF:]
    else:
        k_buf, v_buf = kv_buf
        k_all = jnp.concatenate([k_buf.astype(k.dtype), k], axis=1)
        v_all = jnp.concatenate([v_buf.astype(v.dtype), v], axis=1)
        q_pos = PAST_LEN + jnp.arange(t_, dtype=jnp.int32)
        k_pos = jnp.concatenate([PAST_LEN - WIN_BUF + jnp.arange(WIN_BUF, dtype=jnp.int32), q_pos])
        o = swa_sink_attend(q[:, None], k_all[:, None], v_all[:, None], q_pos[None], k_pos[None], sinks)
        new_k = k_all[:, t_:]
        new_v = v_all[:, t_:]
    o = o.reshape(b_, t_, ATT_WIDTH).astype(x.dtype)
    out_d = (o * jax.nn.silu(att_gate)) @ w_out_att.astype(x.dtype)

    g = jax.nn.sigmoid(merge.reshape(b_, t_, N_BRANCH, D_MODEL))
    h = g[:, :, 0] * out_a + g[:, :, 1] * out_b + g[:, :, 2] * out_c + g[:, :, 3] * out_d
    x = x + h @ w_o.astype(x.dtype)
    return x, (new_h.astype(h_ssm.dtype), new_buf_ssm, new_buf_cf, new_buf_sc, new_k, new_v)


def setup_inputs(seed: int = 0) -> dict:
    key = jax.random.key(seed)
    ks = jax.random.split(key, 32)
    f32 = jnp.float32

    def nrm(k, shape, scale):
        return jax.random.normal(k, shape, f32) * scale

    dt0 = jnp.exp(jax.random.uniform(ks[12], (DEPTH, SSM_HEADS), f32, math.log(1e-3), math.log(1e-1)))
    return {
        "x_prompt": nrm(ks[0], (BATCH, SEQ, D_MODEL), 1.0),
        "x_sample": nrm(ks[1], (DEC_BATCH, DEC_SEQ, D_MODEL), 1.0),
        "state_ssm": nrm(ks[2], (DEPTH, DEC_BATCH, SSM_HEADS, SSM_HEADDIM, SSM_STATE), 0.5),
        "state_conv_ssm": nrm(ks[3], (DEPTH, DEC_BATCH, SSM_CONV - 1, SSM_CONV_DIM), 1.0),
        "state_conv_cf": nrm(ks[4], (DEPTH, DEC_BATCH, CF_CONV_WIDTH - 1, CF_WIDTH), 1.0),
        "state_conv_sc": nrm(ks[5], (DEPTH, DEC_BATCH, SC_CONV_WIDTH - 1, SC_WIDTH), 1.0),
        "cache_k": nrm(ks[6], (DEPTH, DEC_BATCH, WIN_BUF, ATT_KV_HEADS, ATT_HEAD_DIM), 1.0),
        "cache_v": nrm(ks[7], (DEPTH, DEC_BATCH, WIN_BUF, ATT_KV_HEADS, ATT_HEAD_DIM), 1.0),
        "norm_w": 1.0 + nrm(ks[8], (DEPTH, D_MODEL), 0.02),
        "w_in": nrm(ks[9], (DEPTH, D_MODEL, IN_COLS), D_MODEL ** -0.5),
        "ssm_conv_w": nrm(ks[10], (DEPTH, SSM_CONV, SSM_CONV_DIM), SSM_CONV ** -0.5),
        "ssm_conv_b": nrm(ks[11], (DEPTH, SSM_CONV_DIM), 0.02),
        "ssm_dt_bias": dt0 + jnp.log(-jnp.expm1(-dt0)),
        "ssm_a_log": jnp.log(jax.random.uniform(ks[13], (DEPTH, SSM_HEADS), f32, 1.0, 16.0)),
        "ssm_d": 1.0 + nrm(ks[14], (DEPTH, SSM_HEADS), 0.1),
        "ssm_norm_w": 1.0 + nrm(ks[15], (DEPTH, SSM_INNER), 0.02),
        "w_out_ssm": nrm(ks[16], (DEPTH, SSM_INNER, D_MODEL), SSM_INNER ** -0.5),
        "cf_conv_w": nrm(ks[17], (DEPTH, CF_CONV_WIDTH, CF_WIDTH), CF_CONV_WIDTH ** -0.5),
        "cf_conv_b": nrm(ks[18], (DEPTH, CF_WIDTH), 0.02),
        "cf_ln_w": 1.0 + nrm(ks[19], (DEPTH, CF_WIDTH), 0.02),
        "cf_ln_b": nrm(ks[20], (DEPTH, CF_WIDTH), 0.02),
        "w_out_cf": nrm(ks[21], (DEPTH, CF_WIDTH, D_MODEL), CF_WIDTH ** -0.5),
        "sc_conv_w": nrm(ks[22], (DEPTH, SC_CONV_WIDTH, SC_WIDTH), SC_CONV_WIDTH ** -0.5),
        "w_out_sc": nrm(ks[23], (DEPTH, SC_WIDTH, D_MODEL), SC_WIDTH ** -0.5),
        "att_sinks": nrm(ks[24], (DEPTH, ATT_HEADS), 0.5),
        "w_out_att": nrm(ks[25], (DEPTH, ATT_WIDTH, D_MODEL), ATT_WIDTH ** -0.5),
        "w_o": nrm(ks[26], (DEPTH, D_MODEL, D_MODEL), D_MODEL ** -0.5),
        "final_norm_w": 1.0 + nrm(ks[27], (D_MODEL,), 0.02),
    }


def reference(x_prompt, x_sample, state_ssm, state_conv_ssm, state_conv_cf, state_conv_sc,
              cache_k, cache_v, norm_w, w_in, ssm_conv_w, ssm_conv_b, ssm_dt_bias, ssm_a_log,
              ssm_d, ssm_norm_w, w_out_ssm, cf_conv_w, cf_conv_b, cf_ln_w, cf_ln_b, w_out_cf,
              sc_conv_w, w_out_sc, att_sinks, w_out_att, w_o, final_norm_w):
    layer_params = [(norm_w[l], w_in[l], ssm_conv_w[l], ssm_conv_b[l], ssm_dt_bias[l], ssm_a_log[l],
                     ssm_d[l], ssm_norm_w[l], w_out_ssm[l], cf_conv_w[l], cf_conv_b[l], cf_ln_w[l],
                     cf_ln_b[l], w_out_cf[l], sc_conv_w[l], w_out_sc[l], att_sinks[l], w_out_att[l], w_o[l])
                    for l in range(DEPTH)]

    bp = x_prompt.shape[0]
    dtp = x_prompt.dtype
    xp = x_prompt
    p_states = []
    for l in range(DEPTH):
        xp, st = decoder_layer(
            xp,
            jnp.zeros((bp, SSM_HEADS, SSM_HEADDIM, SSM_STATE), dtp),
            jnp.zeros((bp, SSM_CONV - 1, SSM_CONV_DIM), dtp),
            jnp.zeros((bp, CF_CONV_WIDTH - 1, CF_WIDTH), dtp),
            jnp.zeros((bp, SC_CONV_WIDTH - 1, SC_WIDTH), dtp),
            None, layer_params[l], True)
        p_states.append(st)
    y_prompt = rms_norm(xp, final_norm_w)

    xs = x_sample
    s_states = []
    for l in range(DEPTH):
        xs, st = decoder_layer(xs, state_ssm[l], state_conv_ssm[l], state_conv_cf[l], state_conv_sc[l],
                               (cache_k[l], cache_v[l]), layer_params[l], False)
        s_states.append(st)
    y_sample = rms_norm(xs, final_norm_w)

    p_ssm, p_conv_ssm, p_conv_cf, p_conv_sc, p_k, p_v = [jnp.stack(s, axis=0) for s in zip(*p_states)]
    s_ssm, s_conv_ssm, s_conv_cf, s_conv_sc, s_k, s_v = [jnp.stack(s, axis=0) for s in zip(*s_states)]
    return (y_prompt, y_sample, p_ssm, p_conv_ssm, p_conv_cf, p_conv_sc, p_k, p_v,
            s_ssm, s_conv_ssm, s_conv_cf, s_conv_sc, s_k, s_v)
```

```python
import functools

import jax
import jax.numpy as jnp
from jax import lax
from jax.experimental import pallas as pl
from jax.experimental.pallas import tpu as pltpu

F32 = jnp.float32
BF16 = jnp.bfloat16

D_MODEL = 2048
DEPTH = 2
N_BRANCH = 4
WIDTH = 1024
SSM_HEADS = 16
SSM_HEADDIM = 64
SSM_GROUPS = 4
SSM_STATE = 128
SSM_CONV = 4
SSM_CHUNK = 128
CF_CONV = 31
SC_CONV = 3
ATT_HEADS = 16
ATT_KV_HEADS = 4
ATT_HEAD_DIM = 64
ATT_KV_WIDTH = ATT_KV_HEADS * ATT_HEAD_DIM
WINDOW = 128
EPS = 1e-6

LANES = 128
SUBLANES = 8
VMEM_LIMIT = 56 * 1024 * 1024

COL_MERGE = 0
COL_Z = 8
COL_XS = 9
COL_BC = 10
COL_CF_A = 11
COL_CF_G = 12
COL_CF_GATE = 13
COL_SC_B = 14
COL_SC_C = 15
COL_SC_V = 16
COL_SC_GATE = 17
COL_Q = 18
COL_ATT_GATE = 19
COL_K = 80
COL_V = 81
PROJ_COLS = 20992


def _sigmoid(x):
    return 1.0 / (1.0 + jnp.exp(-x))


def _silu(x):
    return x * _sigmoid(x)


def _cparams(sem, limit=VMEM_LIMIT):
    return pltpu.CompilerParams(dimension_semantics=sem, vmem_limit_bytes=limit)


def _rmsnorm_kernel(x_ref, w_ref, o_ref, *, eps):
    x = x_ref[...].astype(F32)
    ms = jnp.mean(x * x, axis=-1, keepdims=True)
    o_ref[...] = (x * lax.rsqrt(ms + eps) * w_ref[...]).astype(o_ref.dtype)


def rmsnorm(x, w, eps, out_dtype, row_off=0, rows=None, tm=512):
    m, d = x.shape
    rows = m if rows is None else rows
    ob = row_off // tm
    return pl.pallas_call(
        functools.partial(_rmsnorm_kernel, eps=eps),
        grid=(rows // tm,),
        in_specs=[pl.BlockSpec((tm, d), lambda i: (i + ob, 0)),
                  pl.BlockSpec((1, d), lambda i: (0, 0))],
        out_specs=pl.BlockSpec((tm, d), lambda i: (i, 0)),
        out_shape=jax.ShapeDtypeStruct((rows, d), out_dtype),
        compiler_params=_cparams(("parallel",)),
        name="rmsnorm",
    )(x, w.reshape(1, d))


def _mm_kernel(a_ref, b_ref, o_ref):
    o_ref[...] = jnp.dot(a_ref[...], b_ref[...], preferred_element_type=F32).astype(o_ref.dtype)


def matmul(a, b, out_dtype, tm, tn, name):
    m, k = a.shape
    n = b.shape[1]
    return pl.pallas_call(
        _mm_kernel,
        grid=(m // tm, n // tn),
        in_specs=[pl.BlockSpec((tm, k), lambda i, j: (i, 0)),
                  pl.BlockSpec((k, tn), lambda i, j: (0, j))],
        out_specs=pl.BlockSpec((tm, tn), lambda i, j: (i, j)),
        out_shape=jax.ShapeDtypeStruct((m, n), out_dtype),
        compiler_params=_cparams(("parallel", "arbitrary")),
        name=name,
    )(a, b)


def _mm_res_kernel(a_ref, b_ref, r_ref, o_ref):
    o_ref[...] = r_ref[...] + jnp.dot(a_ref[...], b_ref[...], preferred_element_type=F32)


def matmul_residual(a, b, r, tm, tn):
    m, k = a.shape
    n = b.shape[1]
    return pl.pallas_call(
        _mm_res_kernel,
        grid=(m // tm, n // tn),
        in_specs=[pl.BlockSpec((tm, k), lambda i, j: (i, 0)),
                  pl.BlockSpec((k, tn), lambda i, j: (0, j)),
                  pl.BlockSpec((tm, tn), lambda i, j: (i, j))],
        out_specs=pl.BlockSpec((tm, tn), lambda i, j: (i, j)),
        out_shape=jax.ShapeDtypeStruct((m, n), F32),
        compiler_params=_cparams(("parallel", "arbitrary")),
        name="out_proj_residual",
    )(a, b, r)


def _merge_kernel(ba_ref, bb_ref, bc_ref, bd_ref, w_ref, m0_ref, m1_ref, m2_ref, m3_ref, o_ref):
    acc = None
    for i, (b_ref, m_ref) in enumerate(((ba_ref, m0_ref), (bb_ref, m1_ref),
                                        (bc_ref, m2_ref), (bd_ref, m3_ref))):
        out = jnp.dot(b_ref[...], w_ref[i], preferred_element_type=F32)
        term = _sigmoid(m_ref[...].astype(F32)) * out
        acc = term if acc is None else acc + term
    o_ref[...] = acc.astype(o_ref.dtype)


def merge_branches(branches, w_out, proj, tm, tn):
    m = proj.shape[0]
    nb = D_MODEL // tn
    br_spec = pl.BlockSpec((tm, WIDTH), lambda i, j: (i, 0))
    merge_specs = [pl.BlockSpec((tm, tn), functools.partial(lambda i, j, b: (i, b * nb + j), b=b))
                   for b in range(N_BRANCH)]
    return pl.pallas_call(
        _merge_kernel,
        grid=(m // tm, nb),
        in_specs=[br_spec] * 4 + [pl.BlockSpec((N_BRANCH, WIDTH, tn), lambda i, j: (0, 0, j))] + merge_specs,
        out_specs=pl.BlockSpec((tm, tn), lambda i, j: (i, j)),
        out_shape=jax.ShapeDtypeStruct((m, D_MODEL), BF16),
        compiler_params=_cparams(("parallel", "arbitrary")),
        name="merge_branches",
    )(*branches, w_out, proj, proj, proj, proj)


def _hdot(a, b):
    return jnp.dot(a, b, precision=lax.Precision.HIGHEST, preferred_element_type=F32)


def _pad_rows(x, rows):
    if x.shape[0] == rows:
        return x
    return jnp.concatenate([x, jnp.zeros((rows - x.shape[0], x.shape[1]), x.dtype)], axis=0)


def _ssd_kernel(*refs, L, nc, aliased):
    (xs_ref, bc_ref, z_ref, dt_ref, pre_ref, h0_ref, cw_ref, cb_ref, dtb_ref, alog_ref,
     dsk_ref, nw_ref, tri_ref, e_ref) = refs[:14]
    rest = refs[14 + (1 if aliased else 0):]
    y_ref, hout_ref, xp_ref, h_ref = rest
    c = pl.program_id(1)
    mm = BF16 if L >= 16 else F32
    off = SUBLANES - (SSM_CONV - 1)

    @pl.when(c == 0)
    def _():
        xp_ref[off:SUBLANES, :] = pre_ref[0]
        h_ref[...] = h0_ref[0]

    if nc > 1:
        @pl.when(c > 0)
        def _():
            xp_ref[off:SUBLANES, :] = xp_ref[L + off:L + SUBLANES, :]

    xp_ref[SUBLANES:SUBLANES + L, 0:WIDTH] = xs_ref[...].astype(F32)
    xp_ref[SUBLANES:SUBLANES + L, WIDTH:2 * WIDTH] = bc_ref[...].astype(F32)

    conv = cb_ref[...] + cw_ref[0:1, :] * xp_ref[off:off + L, :]
    for k in range(1, SSM_CONV):
        conv = conv + cw_ref[k:k + 1, :] * xp_ref[off + k:off + k + L, :]
    xbc = _silu(conv)
    xs = xbc[:, 0:WIDTH]
    bmat = xbc[:, WIDTH:WIDTH + SSM_GROUPS * SSM_STATE]
    cmat = xbc[:, WIDTH + SSM_GROUPS * SSM_STATE:]

    dtr = dt_ref[...] + dtb_ref[...]
    dt = jnp.maximum(dtr, 0.0) + jnp.log1p(jnp.exp(-jnp.abs(dtr)))
    da = dt * (-jnp.exp(alog_ref[...]))
    if L == SSM_CHUNK:
        cs = _hdot(tri_ref[...], da)
    else:
        row = lax.broadcasted_iota(jnp.int32, (L, LANES), 0)
        cs = jnp.zeros((L, LANES), F32)
        for s in range(L):
            cs = cs + jnp.where(row >= s, da[s:s + 1, :], 0.0)
    cs_last = cs[L - 1:L, :]
    ecs = jnp.exp(cs)
    dec = jnp.exp(cs_last - cs)
    e = e_ref[...]
    dt_e = _hdot(dt, e)
    ecs_e = _hdot(ecs, e)
    dec_e = _hdot(dec, e)
    xdt = xs * dt_e
    xdd = xdt * dec_e

    lp = SSM_CHUNK
    xdt_p = _pad_rows(xdt, lp).astype(mm)
    xdd_p = _pad_rows(xdd, lp)
    b_p = _pad_rows(bmat, lp).astype(mm)
    cs_t = _pad_rows(cs, lp).T
    cmat_m = cmat.astype(mm)

    li = lax.broadcasted_iota(jnp.int32, (L, lp), 0)
    sj = lax.broadcasted_iota(jnp.int32, (L, lp), 1)
    causal = sj <= li
    lane = lax.broadcasted_iota(jnp.int32, (lp, LANES), 1)
    lo_half = lane < SSM_HEADDIM

    y_blocks = []
    hpg = SSM_HEADS // SSM_GROUPS
    gw = hpg * SSM_HEADDIM
    for g in range(SSM_GROUPS):
        c_g = cmat_m[:, g * SSM_STATE:(g + 1) * SSM_STATE]
        b_g = b_p[:, g * SSM_STATE:(g + 1) * SSM_STATE]
        cb = lax.dot_general(c_g, b_g, (((1,), (1,)), ((), ())), preferred_element_type=F32)
        h_blk = h_ref[g * gw:(g + 1) * gw, :]
        y_off = lax.dot_general(c_g, h_blk.astype(mm), (((1,), (1,)), ((), ())),
                                preferred_element_type=F32)
        y_off = y_off * ecs_e[:, g * gw:(g + 1) * gw]
        for jb in range(gw // LANES):
            blk = g * (gw // LANES) + jb
            x_blk = xdt_p[:, blk * LANES:(blk + 1) * LANES]
            acc = None
            for par in range(2):
                h = 2 * blk + par
                diff = cs[:, h:h + 1] - cs_t[h:h + 1, :]
                lm = jnp.exp(jnp.where(causal, diff, -jnp.inf))
                gmat = (cb * lm).astype(mm)
                x_half = jnp.where(lo_half if par == 0 else jnp.logical_not(lo_half), x_blk,
                                   jnp.zeros_like(x_blk))
                t = jnp.dot(gmat, x_half, preferred_element_type=F32)
                acc = t if acc is None else acc + t
            y_blocks.append(acc + y_off[:, jb * LANES:(jb + 1) * LANES])
        xdd_t = jnp.concatenate(
            [xdd_p[:, (g * (gw // LANES) + jb) * LANES:(g * (gw // LANES) + jb + 1) * LANES].T
             for jb in range(gw // LANES)], axis=0).astype(mm)
        s_new = jnp.dot(xdd_t, b_g, preferred_element_type=F32)
        for r in range(hpg):
            h = g * hpg + r
            scal = jnp.exp(cs_t[h:h + 1, L - 1:L])
            rows = slice(h * SSM_HEADDIM, (h + 1) * SSM_HEADDIM)
            h_ref[rows, :] = h_ref[rows, :] * scal + s_new[r * SSM_HEADDIM:(r + 1) * SSM_HEADDIM, :]

    y = jnp.concatenate(y_blocks, axis=1) + dsk_ref[...] * xs
    y = y * _silu(z_ref[...].astype(F32))
    ms = jnp.mean(y * y, axis=-1, keepdims=True)
    y_ref[...] = (y * lax.rsqrt(ms + 1e-5) * nw_ref[...]).astype(y_ref.dtype)

    @pl.when(c == nc - 1)
    def _():
        hout_ref[0] = h_ref[...]


def ssd_branch(proj, dtr, prefix, h0, lp, consts, *, nbatch, t, row_off, total_rows, y_prev=None):
    L = SSM_CHUNK if t % SSM_CHUNK == 0 else t
    nc = t // L
    rb = row_off // L

    def rows(b, c):
        return rb + b * nc + c

    def col(cidx):
        return pl.BlockSpec((L, WIDTH), lambda b, c: (rows(b, c), cidx))

    def const(shape):
        return pl.BlockSpec(shape, lambda b, c: (0,) * len(shape))

    tri, e = consts
    in_specs = [col(COL_XS), col(COL_BC), col(COL_Z),
                pl.BlockSpec((L, LANES), lambda b, c: (rows(b, c), 0)),
                pl.BlockSpec((1, SSM_CONV - 1, 2 * WIDTH), lambda b, c: (b, 0, 0)),
                pl.BlockSpec((1, WIDTH, SSM_STATE), lambda b, c: (b, 0, 0)),
                const((SSM_CONV, 2 * WIDTH)), const((1, 2 * WIDTH)), const((1, LANES)),
                const((1, LANES)), const((1, WIDTH)), const((1, WIDTH)),
                const((SSM_CHUNK, SSM_CHUNK)), const((LANES, WIDTH))]
    args = [proj, proj, proj, dtr, prefix, h0, lp["ssm_conv_w"], lp["ssm_conv_b"], lp["dt_bias"],
            lp["a_log"], lp["d_skip"], lp["ssm_norm_w"], tri, e]
    aliases = {}
    if y_prev is not None:
        in_specs.append(pl.BlockSpec(memory_space=pl.ANY))
        args.append(y_prev)
        aliases = {len(args) - 1: 0}
    return pl.pallas_call(
        functools.partial(_ssd_kernel, L=L, nc=nc, aliased=y_prev is not None),
        grid=(nbatch, nc),
        in_specs=in_specs,
        out_specs=[pl.BlockSpec((L, WIDTH), lambda b, c: (rows(b, c), 0)),
                   pl.BlockSpec((1, WIDTH, SSM_STATE), lambda b, c: (b, 0, 0))],
        out_shape=[jax.ShapeDtypeStruct((total_rows, WIDTH), BF16),
                   jax.ShapeDtypeStruct((nbatch, WIDTH, SSM_STATE), F32)],
        scratch_shapes=[pltpu.VMEM((SUBLANES + L, 2 * WIDTH), F32),
                        pltpu.VMEM((WIDTH, SSM_STATE), F32)],
        input_output_aliases=aliases,
        compiler_params=_cparams(("parallel", "arbitrary")),
        name="ssd_branch",
    )(*args)


def _dwconv(xp_ref, w_ref, b_ref, acc_ref, *, k, start, tb):
    rbk = min(tb, 32)
    for j in range(WIDTH // LANES):
        lanes = slice(j * LANES, (j + 1) * LANES)
        for r in range(tb // rbk):
            r0 = r * rbk
            acc = w_ref[0:1, lanes] * xp_ref[start + r0:start + r0 + rbk, lanes]
            if b_ref is not None:
                acc = acc + b_ref[:, lanes]
            for i in range(1, k):
                acc = acc + w_ref[i:i + 1, lanes] * xp_ref[start + r0 + i:start + r0 + i + rbk, lanes]
            acc_ref[r0:r0 + rbk, lanes] = acc


def _conv_carry(xp_ref, pre_ref, c, *, k, p, tb, nc):
    @pl.when(c == 0)
    def _():
        xp_ref[p - (k - 1):p, :] = pre_ref[0]

    if nc > 1:
        @pl.when(c > 0)
        def _():
            xp_ref[0:p, :] = xp_ref[tb:tb + p, :]


def _cf_kernel(*refs, tb, nc, p, aliased):
    a_ref, g_ref, gate_ref, pre_ref, w_ref, b_ref, lnw_ref, lnb_ref = refs[:8]
    o_ref, st_ref, xp_ref, acc_ref = refs[8 + (1 if aliased else 0):]
    c = pl.program_id(1)
    k = CF_CONV
    _conv_carry(xp_ref, pre_ref, c, k=k, p=p, tb=tb, nc=nc)
    xp_ref[p:p + tb, :] = a_ref[...].astype(F32) * _sigmoid(g_ref[...].astype(F32))
    _dwconv(xp_ref, w_ref, b_ref, acc_ref, k=k, start=p - (k - 1), tb=tb)
    u = acc_ref[...]
    mu = jnp.mean(u, axis=-1, keepdims=True)
    uc = u - mu
    var = jnp.mean(uc * uc, axis=-1, keepdims=True)
    u = _silu(uc * lax.rsqrt(var + 1e-5) * lnw_ref[...] + lnb_ref[...])
    o_ref[...] = (u * _silu(gate_ref[...].astype(F32))).astype(o_ref.dtype)

    @pl.when(c == nc - 1)
    def _():
        st_ref[0] = xp_ref[p + tb - (k - 1):p + tb, :]


def _sc_kernel(*refs, tb, nc, p, aliased):
    gb_ref, gc_ref, sv_ref, gate_ref, pre_ref, w_ref = refs[:6]
    o_ref, st_ref, xp_ref, acc_ref = refs[6 + (1 if aliased else 0):]
    c = pl.program_id(1)
    k = SC_CONV
    _conv_carry(xp_ref, pre_ref, c, k=k, p=p, tb=tb, nc=nc)
    xp_ref[p:p + tb, :] = gc_ref[...].astype(F32) * sv_ref[...].astype(F32)
    _dwconv(xp_ref, w_ref, None, acc_ref, k=k, start=p - (k - 1), tb=tb)
    o_ref[...] = (gb_ref[...].astype(F32) * acc_ref[...]
                  * _silu(gate_ref[...].astype(F32))).astype(o_ref.dtype)

    @pl.when(c == nc - 1)
    def _():
        st_ref[0] = xp_ref[p + tb - (k - 1):p + tb, :]


def conv_branch(kind, proj, prefix, lp, *, nbatch, t, row_off, total_rows, y_prev=None):
    tb = min(t, 128)
    nc = t // tb
    rb = row_off // tb

    def rows(b, c):
        return rb + b * nc + c

    def col(cidx):
        return pl.BlockSpec((tb, WIDTH), lambda b, c: (rows(b, c), cidx))

    def const(shape):
        return pl.BlockSpec(shape, lambda b, c: (0,) * len(shape))

    if kind == "cf":
        k, p = CF_CONV, 32
        in_specs = [col(COL_CF_A), col(COL_CF_G), col(COL_CF_GATE),
                    pl.BlockSpec((1, k - 1, WIDTH), lambda b, c: (b, 0, 0)),
                    const((k, WIDTH)), const((1, WIDTH)), const((1, WIDTH)), const((1, WIDTH))]
        args = [proj, proj, proj, prefix, lp["cf_conv_w"], lp["cf_conv_b"], lp["cf_ln_w"], lp["cf_ln_b"]]
        body = _cf_kernel
    else:
        k, p = SC_CONV, SUBLANES
        in_specs = [col(COL_SC_B), col(COL_SC_C), col(COL_SC_V), col(COL_SC_GATE),
                    pl.BlockSpec((1, k - 1, WIDTH), lambda b, c: (b, 0, 0)),
                    const((k, WIDTH))]
        args = [proj, proj, proj, proj, prefix, lp["sc_conv_w"]]
        body = _sc_kernel
    aliases = {}
    if y_prev is not None:
        in_specs.append(pl.BlockSpec(memory_space=pl.ANY))
        args.append(y_prev)
        aliases = {len(args) - 1: 0}
    return pl.pallas_call(
        functools.partial(body, tb=tb, nc=nc, p=p, aliased=y_prev is not None),
        grid=(nbatch, nc),
        in_specs=in_specs,
        out_specs=[pl.BlockSpec((tb, WIDTH), lambda b, c: (rows(b, c), 0)),
                   pl.BlockSpec((1, k - 1, WIDTH), lambda b, c: (b, 0, 0))],
        out_shape=[jax.ShapeDtypeStruct((total_rows, WIDTH), BF16),
                   jax.ShapeDtypeStruct((nbatch, k - 1, WIDTH), F32)],
        scratch_shapes=[pltpu.VMEM((p + tb, WIDTH), F32), pltpu.VMEM((tb, WIDTH), F32)],
        input_output_aliases=aliases,
        compiler_params=_cparams(("parallel", "arbitrary")),
        name=kind + "_branch",
    )(*args)


def _attn_core(q, k2, v2, gate, sink_ref, ok, *, L, mm):
    nkeys = k2.shape[0]
    lane = lax.broadcasted_iota(jnp.int32, (nkeys, LANES), 1)
    lo_half = lane < ATT_HEAD_DIM
    rep = ATT_HEADS // ATT_KV_HEADS
    o_blocks = [None] * (WIDTH // LANES)
    for g in range(ATT_KV_HEADS):
        sl = slice((g // 2) * LANES, (g // 2 + 1) * LANES)
        kb, vb = k2[:, sl], v2[:, sl]
        kr, vr = pltpu.roll(kb, ATT_HEAD_DIM, 1), pltpu.roll(vb, ATT_HEAD_DIM, 1)
        zero = jnp.zeros_like(kb)
        if g % 2 == 0:
            k_lo, k_hi = jnp.where(lo_half, kb, zero), jnp.where(lo_half, zero, kr)
            v_lo, v_hi = jnp.where(lo_half, vb, zero), jnp.where(lo_half, zero, vr)
        else:
            k_lo, k_hi = jnp.where(lo_half, kr, zero), jnp.where(lo_half, zero, kb)
            v_lo, v_hi = jnp.where(lo_half, vr, zero), jnp.where(lo_half, zero, vb)
        k_lo, k_hi, v_lo, v_hi = (a.astype(mm) for a in (k_lo, k_hi, v_lo, v_hi))
        for r in range(rep):
            h = g * rep + r
            blk, par = h // 2, h % 2
            q_blk = q[:, blk * LANES:(blk + 1) * LANES].astype(mm)
            s = lax.dot_general(q_blk, k_lo if par == 0 else k_hi, (((1,), (1,)), ((), ())),
                                preferred_element_type=F32)
            s = jnp.where(ok, s, -jnp.inf)
            sink = sink_ref[h]
            m = jnp.maximum(jnp.max(s, axis=-1, keepdims=True), sink)
            p = jnp.exp(s - m)
            denom = jnp.sum(p, axis=-1, keepdims=True) + jnp.exp(sink - m)
            pn = (p / denom).astype(mm)
            t = jnp.dot(pn, v_lo if par == 0 else v_hi, preferred_element_type=F32)
            o_blocks[blk] = t if o_blocks[blk] is None else o_blocks[blk] + t
    o = jnp.concatenate(o_blocks, axis=1)
    return o * _silu(gate)


def _attn_prompt_kernel(q_ref, kp_ref, kc_ref, vp_ref, vc_ref, gate_ref, sink_ref, o_ref):
    L = WINDOW
    i = pl.program_id(1)
    q = q_ref[...].astype(F32) * (ATT_HEAD_DIM ** -0.5)
    k2 = jnp.concatenate([kp_ref[...], kc_ref[...]], axis=0).astype(F32)
    v2 = jnp.concatenate([vp_ref[...], vc_ref[...]], axis=0).astype(F32)
    li = lax.broadcasted_iota(jnp.int32, (L, 2 * L), 0)
    sj = lax.broadcasted_iota(jnp.int32, (L, 2 * L), 1)
    ok = (sj > li) & (sj <= li + WINDOW) & ((sj >= L) | (i > 0))
    o = _attn_core(q, k2, v2, gate_ref[...].astype(F32), sink_ref, ok, L=L, mm=BF16)
    o_ref[...] = o.astype(o_ref.dtype)


def _attn_sample_kernel(q_ref, kn_ref, vn_ref, ck_ref, cv_ref, gate_ref, sink_ref, yprev_ref,
                        o_ref, kw_ref, vw_ref, *, L):
    del yprev_ref
    q = q_ref[...].astype(F32) * (ATT_HEAD_DIM ** -0.5)
    pad = jnp.zeros((WINDOW - L, ATT_KV_WIDTH), F32)
    k2 = jnp.concatenate([ck_ref[0], kn_ref[...].astype(F32), pad], axis=0)
    v2 = jnp.concatenate([cv_ref[0], vn_ref[...].astype(F32), pad], axis=0)
    li = lax.broadcasted_iota(jnp.int32, (L, 2 * WINDOW), 0)
    sj = lax.broadcasted_iota(jnp.int32, (L, 2 * WINDOW), 1)
    ok = (sj > li) & (sj <= li + WINDOW)
    o = _attn_core(q, k2, v2, gate_ref[...].astype(F32), sink_ref, ok, L=L, mm=F32)
    o_ref[...] = o.astype(o_ref.dtype)
    kw_ref[0] = k2[L:L + WINDOW, :]
    vw_ref[0] = v2[L:L + WINDOW, :]


def attn_prompt(proj, sinks, *, nbatch, t, total_rows):
    L = WINDOW
    nblk = t // L

    def col(cidx):
        return pl.BlockSpec((L, WIDTH), lambda b, i: (b * nblk + i, cidx))

    def kv(cidx, prev):
        if prev:
            return pl.BlockSpec((L, ATT_KV_WIDTH), lambda b, i: (b * nblk + jnp.maximum(i - 1, 0), cidx))
        return pl.BlockSpec((L, ATT_KV_WIDTH), lambda b, i: (b * nblk + i, cidx))

    return pl.pallas_call(
        _attn_prompt_kernel,
        grid=(nbatch, nblk),
        in_specs=[col(COL_Q), kv(COL_K, True), kv(COL_K, False), kv(COL_V, True), kv(COL_V, False),
                  col(COL_ATT_GATE), pl.BlockSpec(memory_space=pltpu.SMEM)],
        out_specs=pl.BlockSpec((L, WIDTH), lambda b, i: (b * nblk + i, 0)),
        out_shape=jax.ShapeDtypeStruct((total_rows, WIDTH), BF16),
        compiler_params=_cparams(("parallel", "arbitrary")),
        name="attn_prompt",
    )(proj, proj, proj, proj, proj, proj, sinks)


def attn_sample(proj, sinks, cache_k, cache_v, y_prev, *, nbatch, t, row_off):
    rb = row_off // t

    def col(cidx, w):
        return pl.BlockSpec((t, w), lambda b: (rb + b, cidx))

    cache = pl.BlockSpec((1, WINDOW, ATT_KV_WIDTH), lambda b: (b, 0, 0))
    return pl.pallas_call(
        functools.partial(_attn_sample_kernel, L=t),
        grid=(nbatch,),
        in_specs=[col(COL_Q, WIDTH), col(COL_K, ATT_KV_WIDTH), col(COL_V, ATT_KV_WIDTH), cache, cache,
                  col(COL_ATT_GATE, WIDTH), pl.BlockSpec(memory_space=pltpu.SMEM),
                  pl.BlockSpec(memory_space=pl.ANY)],
        out_specs=[pl.BlockSpec((t, WIDTH), lambda b: (rb + b, 0)), cache, cache],
        out_shape=[jax.ShapeDtypeStruct(y_prev.shape, BF16),
                   jax.ShapeDtypeStruct((nbatch, WINDOW, ATT_KV_WIDTH), F32),
                   jax.ShapeDtypeStruct((nbatch, WINDOW, ATT_KV_WIDTH), F32)],
        input_output_aliases={7: 0},
        compiler_params=_cparams(("parallel",)),
        name="attn_sample",
    )(proj, proj, proj, cache_k, cache_v, proj, sinks, y_prev)


def _prep_layer(l, norm_w, w_in, ssm_conv_w, ssm_conv_b, ssm_dt_bias, ssm_a_log, ssm_d, ssm_norm_w,
                w_out_ssm, cf_conv_w, cf_conv_b, cf_ln_w, cf_ln_b, w_out_cf, sc_conv_w, w_out_sc,
                att_sinks, w_out_att, w_o):
    w = w_in[l]
    o = 0
    seg = {}
    for name, width in (("z", WIDTH), ("xbc", 2 * WIDTH), ("dt", SSM_HEADS), ("cf_in", 2 * WIDTH),
                        ("cf_gate", WIDTH), ("sc_in", 3 * WIDTH), ("sc_gate", WIDTH), ("q", WIDTH),
                        ("k", ATT_KV_WIDTH), ("v", ATT_KV_WIDTH), ("att_gate", WIDTH),
                        ("merge", N_BRANCH * D_MODEL)):
        seg[name] = w[:, o:o + width]
        o += width
    w_proj = jnp.concatenate([seg["merge"], seg["z"], seg["xbc"], seg["cf_in"], seg["cf_gate"],
                              seg["sc_in"], seg["sc_gate"], seg["q"], seg["att_gate"], seg["k"],
                              seg["v"]], axis=1).astype(BF16)
    w_dt = jnp.pad(seg["dt"], ((0, 0), (0, LANES - SSM_HEADS))).astype(BF16)
    pad_h = (0, LANES - SSM_HEADS)
    return dict(
        norm_w=norm_w[l], w_proj=w_proj, w_dt=w_dt,
        ssm_conv_w=ssm_conv_w[l], ssm_conv_b=ssm_conv_b[l].reshape(1, -1),
        dt_bias=jnp.pad(ssm_dt_bias[l], pad_h).reshape(1, LANES),
        a_log=jnp.pad(ssm_a_log[l], pad_h).reshape(1, LANES),
        d_skip=jnp.repeat(ssm_d[l], SSM_HEADDIM).reshape(1, WIDTH),
        ssm_norm_w=ssm_norm_w[l].reshape(1, WIDTH),
        cf_conv_w=cf_conv_w[l], cf_conv_b=cf_conv_b[l].reshape(1, WIDTH),
        cf_ln_w=cf_ln_w[l].reshape(1, WIDTH), cf_ln_b=cf_ln_b[l].reshape(1, WIDTH),
        sc_conv_w=sc_conv_w[l], sinks=att_sinks[l],
        w_out=jnp.stack([w_out_ssm[l], w_out_cf[l], w_out_sc[l], w_out_att[l]]).astype(BF16),
        w_o=w_o[l].astype(BF16),
    )


def kernel(x_prompt, x_sample, state_ssm, state_conv_ssm, state_conv_cf, state_conv_sc, cache_k, cache_v, norm_w, w_in, ssm_conv_w, ssm_conv_b, ssm_dt_bias, ssm_a_log, ssm_d, ssm_norm_w, w_out_ssm, cf_conv_w, cf_conv_b, cf_ln_w, cf_ln_b, w_out_cf, sc_conv_w, w_out_sc, att_sinks, w_out_att, w_o, final_norm_w):
    bp, tp, _ = x_prompt.shape
    bs, ts, _ = x_sample.shape
    mp, ms = bp * tp, bs * ts
    m = mp + ms
    x = jnp.concatenate([x_prompt.reshape(mp, D_MODEL), x_sample.reshape(ms, D_MODEL)], axis=0)

    tri = jnp.tril(jnp.ones((SSM_CHUNK, SSM_CHUNK), F32))
    expand = (jnp.arange(WIDTH)[None, :] // SSM_HEADDIM == jnp.arange(LANES)[:, None]).astype(F32)
    consts = (tri, expand)

    p_states, s_states = [], []
    for l in range(DEPTH):
        lp = _prep_layer(l, norm_w, w_in, ssm_conv_w, ssm_conv_b, ssm_dt_bias, ssm_a_log, ssm_d,
                         ssm_norm_w, w_out_ssm, cf_conv_w, cf_conv_b, cf_ln_w, cf_ln_b, w_out_cf,
                         sc_conv_w, w_out_sc, att_sinks, w_out_att, w_o)
        xn = rmsnorm(x, lp["norm_w"], EPS, BF16)
        proj = matmul(xn, lp["w_proj"], BF16, tm=2304, tn=512, name="in_proj")
        dtr = matmul(xn, lp["w_dt"], F32, tm=1024, tn=LANES, name="dt_proj")

        zeros = lambda *s: jnp.zeros(s, F32)
        kw = dict(nbatch=bp, t=tp, row_off=0, total_rows=m)
        ya, p_h = ssd_branch(proj, dtr, zeros(bp, SSM_CONV - 1, 2 * WIDTH), zeros(bp, WIDTH, SSM_STATE),
                             lp, consts, **kw)
        yb, p_cf = conv_branch("cf", proj, zeros(bp, CF_CONV - 1, WIDTH), lp, **kw)
        yc, p_sc = conv_branch("sc", proj, zeros(bp, SC_CONV - 1, WIDTH), lp, **kw)
        yd = attn_prompt(proj, lp["sinks"], nbatch=bp, t=tp, total_rows=m)
        kw = dict(nbatch=bs, t=ts, row_off=mp, total_rows=m)
        ya, s_h = ssd_branch(proj, dtr, state_conv_ssm[l], state_ssm[l].reshape(bs, WIDTH, SSM_STATE),
                             lp, consts, y_prev=ya, **kw)
        yb, s_cf = conv_branch("cf", proj, state_conv_cf[l], lp, y_prev=yb, **kw)
        yc, s_sc = conv_branch("sc", proj, state_conv_sc[l], lp, y_prev=yc, **kw)
        yd, s_k, s_v = attn_sample(proj, lp["sinks"],
                                   cache_k[l].reshape(bs, WINDOW, ATT_KV_WIDTH),
                                   cache_v[l].reshape(bs, WINDOW, ATT_KV_WIDTH), yd,
                                   nbatch=bs, t=ts, row_off=mp)

        hmix = merge_branches((ya, yb, yc, yd), lp["w_out"], proj, tm=1024, tn=512)
        x = matmul_residual(hmix, lp["w_o"], x, tm=1024, tn=1024)

        xbc0 = COL_XS * WIDTH
        pp = proj[:mp].reshape(bp, tp, PROJ_COLS)
        ps = proj[mp:].reshape(bs, ts, PROJ_COLS)
        k0, v0 = COL_K * ATT_KV_WIDTH, COL_V * ATT_KV_WIDTH
        p_states.append((
            p_h.reshape(bp, SSM_HEADS, SSM_HEADDIM, SSM_STATE),
            pp[:, tp - (SSM_CONV - 1):, xbc0:xbc0 + 2 * WIDTH].astype(F32),
            p_cf, p_sc,
            pp[:, tp - WINDOW:, k0:k0 + ATT_KV_WIDTH].astype(F32).reshape(bp, WINDOW, ATT_KV_HEADS, ATT_HEAD_DIM),
            pp[:, tp - WINDOW:, v0:v0 + ATT_KV_WIDTH].astype(F32).reshape(bp, WINDOW, ATT_KV_HEADS, ATT_HEAD_DIM)))
        s_states.append((
            s_h.reshape(bs, SSM_HEADS, SSM_HEADDIM, SSM_STATE),
            ps[:, ts - (SSM_CONV - 1):, xbc0:xbc0 + 2 * WIDTH].astype(F32),
            s_cf, s_sc,
            s_k.reshape(bs, WINDOW, ATT_KV_HEADS, ATT_HEAD_DIM),
            s_v.reshape(bs, WINDOW, ATT_KV_HEADS, ATT_HEAD_DIM)))

    y_prompt = rmsnorm(x, final_norm_w, EPS, F32, row_off=0, rows=mp).reshape(bp, tp, D_MODEL)
    y_sample = rmsnorm(x, final_norm_w, EPS, F32, row_off=mp, rows=ms).reshape(bs, ts, D_MODEL)
    p_out = [jnp.stack(s, axis=0) for s in zip(*p_states)]
    s_out = [jnp.stack(s, axis=0) for s in zip(*s_states)]
    return (y_prompt, y_sample, *p_out, *s_out)
```

```python
import functools

import jax
import jax.numpy as jnp
from jax import lax
from jax.experimental import pallas as pl
from jax.experimental.pallas import tpu as pltpu

F32 = jnp.float32
BF16 = jnp.bfloat16

D_MODEL = 2048
DEPTH = 2
N_BRANCH = 4
WIDTH = 1024
SSM_HEADS = 16
SSM_HEADDIM = 64
SSM_GROUPS = 4
SSM_STATE = 128
SSM_CONV = 4
SSM_CHUNK = 128
CF_CONV = 31
SC_CONV = 3
ATT_HEADS = 16
ATT_KV_HEADS = 4
ATT_HEAD_DIM = 64
ATT_KV_WIDTH = ATT_KV_HEADS * ATT_HEAD_DIM
ATT_REP = ATT_HEADS // ATT_KV_HEADS
WINDOW = 128
EPS = 1e-6

LANES = 128
SUBLANES = 8
VMEM_LIMIT = 56 * 1024 * 1024

COL_MERGE = 0
COL_Z = 8
COL_XS = 9
COL_BC = 10
COL_CF_A = 11
COL_CF_G = 12
COL_CF_GATE = 13
COL_SC_B = 14
COL_SC_C = 15
COL_SC_V = 16
COL_SC_GATE = 17
COL_Q = 18
COL_ATT_GATE = 19
COL_K = 80
COL_V = 81
PROJ_COLS = 20992


def _sigmoid(x):
    return 1.0 / (1.0 + jnp.exp(-x))


def _silu(x):
    return x * _sigmoid(x)


def _cparams(sem, limit=VMEM_LIMIT):
    return pltpu.CompilerParams(dimension_semantics=sem, vmem_limit_bytes=limit)


def _tile(m, target):
    t = min(m, target)
    while m % t or t % SUBLANES:
        t -= 1
    return t


def _const_spec(shape):
    return pl.BlockSpec(shape, lambda *_: (0,) * len(shape))


def _stacked_out(l, nbatch, nb, tail, prev):
    shape = (DEPTH, nbatch) + tail
    spec = pl.BlockSpec((1, nb) + tail, lambda b, *_: (l, b) + (0,) * len(tail))
    return jax.ShapeDtypeStruct(shape, F32), spec, ([] if prev is None else [prev])


def _rmsnorm_kernel(x_ref, w_ref, o_ref, *, eps):
    x = x_ref[...].astype(F32)
    ms = jnp.mean(x * x, axis=-1, keepdims=True)
    o_ref[...] = (x * lax.rsqrt(ms + eps) * w_ref[...]).astype(o_ref.dtype)


def rmsnorm(x, w, eps, out_dtype):
    m, d = x.shape
    tm = _tile(m, 512)
    return pl.pallas_call(
        functools.partial(_rmsnorm_kernel, eps=eps),
        grid=(m // tm,),
        in_specs=[pl.BlockSpec((tm, d), lambda i: (i, 0)), _const_spec((1, d))],
        out_specs=pl.BlockSpec((tm, d), lambda i: (i, 0)),
        out_shape=jax.ShapeDtypeStruct((m, d), out_dtype),
        compiler_params=_cparams(("parallel",)),
        name="rmsnorm",
    )(x, w.reshape(1, d))


def _mm_kernel(a_ref, b_ref, o_ref):
    o_ref[...] = jnp.dot(a_ref[...], b_ref[...], preferred_element_type=F32).astype(o_ref.dtype)


def matmul(a, b, out_dtype, tm, tn, name):
    m, k = a.shape
    n = b.shape[1]
    tm = _tile(m, tm)
    return pl.pallas_call(
        _mm_kernel,
        grid=(m // tm, n // tn),
        in_specs=[pl.BlockSpec((tm, k), lambda i, j: (i, 0)),
                  pl.BlockSpec((k, tn), lambda i, j: (0, j))],
        out_specs=pl.BlockSpec((tm, tn), lambda i, j: (i, j)),
        out_shape=jax.ShapeDtypeStruct((m, n), out_dtype),
        compiler_params=_cparams(("parallel", "arbitrary")),
        name=name,
    )(a, b)


def _mm_res_kernel(a_ref, b_ref, r_ref, o_ref):
    o_ref[...] = r_ref[...] + jnp.dot(a_ref[...], b_ref[...], preferred_element_type=F32)


def matmul_residual(a, b, r, tm, tn):
    m, k = a.shape
    n = b.shape[1]
    tm = _tile(m, tm)
    return pl.pallas_call(
        _mm_res_kernel,
        grid=(m // tm, n // tn),
        in_specs=[pl.BlockSpec((tm, k), lambda i, j: (i, 0)),
                  pl.BlockSpec((k, tn), lambda i, j: (0, j)),
                  pl.BlockSpec((tm, tn), lambda i, j: (i, j))],
        out_specs=pl.BlockSpec((tm, tn), lambda i, j: (i, j)),
        out_shape=jax.ShapeDtypeStruct((m, n), F32),
        compiler_params=_cparams(("parallel", "arbitrary")),
        name="out_proj_residual",
    )(a, b, r)


def _merge_kernel(ba_ref, bb_ref, bc_ref, bd_ref, w_ref, m0_ref, m1_ref, m2_ref, m3_ref, o_ref):
    acc = None
    for i, (b_ref, m_ref) in enumerate(((ba_ref, m0_ref), (bb_ref, m1_ref),
                                        (bc_ref, m2_ref), (bd_ref, m3_ref))):
        out = jnp.dot(b_ref[...], w_ref[i], preferred_element_type=F32)
        term = _sigmoid(m_ref[...].astype(F32)) * out
        acc = term if acc is None else acc + term
    o_ref[...] = acc.astype(o_ref.dtype)


def merge_branches(branches, w_out, proj, tm, tn):
    m = proj.shape[0]
    tm = _tile(m, tm)
    nb = D_MODEL // tn
    br_spec = pl.BlockSpec((tm, WIDTH), lambda i, j: (i, 0))
    merge_specs = [pl.BlockSpec((tm, tn), functools.partial(lambda i, j, b: (i, b * nb + j), b=b))
                   for b in range(N_BRANCH)]
    return pl.pallas_call(
        _merge_kernel,
        grid=(m // tm, nb),
        in_specs=[br_spec] * 4 + [pl.BlockSpec((N_BRANCH, WIDTH, tn), lambda i, j: (0, 0, j))] + merge_specs,
        out_specs=pl.BlockSpec((tm, tn), lambda i, j: (i, j)),
        out_shape=jax.ShapeDtypeStruct((m, D_MODEL), BF16),
        compiler_params=_cparams(("parallel", "arbitrary")),
        name="merge_branches",
    )(*branches, w_out, proj, proj, proj, proj)


def _hdot(a, b):
    return jnp.dot(a, b, precision=lax.Precision.HIGHEST, preferred_element_type=F32)


def _pad_rows(x, rows):
    if x.shape[0] == rows:
        return x
    return jnp.concatenate([x, jnp.zeros((rows - x.shape[0], x.shape[1]), x.dtype)], axis=0)


def _ssd_core(xbc, z, dtr, h_ref, consts, *, L):
    dtb, alog, dsk, nw, tri, e = consts
    mm = BF16 if L >= 16 else F32
    xs = xbc[:, 0:WIDTH]
    bmat = xbc[:, WIDTH:WIDTH + SSM_GROUPS * SSM_STATE]
    cmat = xbc[:, WIDTH + SSM_GROUPS * SSM_STATE:]

    dtr = dtr + dtb
    dt = jnp.maximum(dtr, 0.0) + jnp.log1p(jnp.exp(-jnp.abs(dtr)))
    da = dt * (-jnp.exp(alog))
    if L == SSM_CHUNK:
        cs = _hdot(tri, da)
    else:
        row = lax.broadcasted_iota(jnp.int32, (L, LANES), 0)
        cs = jnp.zeros((L, LANES), F32)
        for s in range(L):
            cs = cs + jnp.where(row >= s, da[s:s + 1, :], 0.0)
    cs_last = cs[L - 1:L, :]
    ecs = jnp.exp(cs)
    dec = jnp.exp(cs_last - cs)
    dt_e = _hdot(dt, e)
    ecs_e = _hdot(ecs, e)
    dec_e = _hdot(dec, e)
    xdt = xs * dt_e
    xdd = xdt * dec_e

    lp = SSM_CHUNK
    xdt_p = _pad_rows(xdt, lp).astype(mm)
    xdd_p = _pad_rows(xdd, lp)
    b_p = _pad_rows(bmat, lp).astype(mm)
    cs_t = _pad_rows(cs, lp).T
    cmat_m = cmat.astype(mm)

    li = lax.broadcasted_iota(jnp.int32, (L, lp), 0)
    sj = lax.broadcasted_iota(jnp.int32, (L, lp), 1)
    causal = sj <= li
    lane = lax.broadcasted_iota(jnp.int32, (lp, LANES), 1)
    lo_half = lane < SSM_HEADDIM

    y_blocks = []
    hpg = SSM_HEADS // SSM_GROUPS
    gw = hpg * SSM_HEADDIM
    bpg = gw // LANES
    for g in range(SSM_GROUPS):
        c_g = cmat_m[:, g * SSM_STATE:(g + 1) * SSM_STATE]
        b_g = b_p[:, g * SSM_STATE:(g + 1) * SSM_STATE]
        cb = lax.dot_general(c_g, b_g, (((1,), (1,)), ((), ())), preferred_element_type=F32)
        h_blk = h_ref[g * gw:(g + 1) * gw, :]
        y_off = lax.dot_general(c_g, h_blk.astype(mm), (((1,), (1,)), ((), ())),
                                preferred_element_type=F32)
        y_off = y_off * ecs_e[:, g * gw:(g + 1) * gw]
        for jb in range(bpg):
            blk = g * bpg + jb
            x_blk = xdt_p[:, blk * LANES:(blk + 1) * LANES]
            acc = None
            for par in range(2):
                h = 2 * blk + par
                diff = cs[:, h:h + 1] - cs_t[h:h + 1, :]
                lm = jnp.exp(jnp.where(causal, diff, -jnp.inf))
                gmat = (cb * lm).astype(mm)
                x_half = jnp.where(lo_half if par == 0 else jnp.logical_not(lo_half), x_blk,
                                   jnp.zeros_like(x_blk))
                t = jnp.dot(gmat, x_half, preferred_element_type=F32)
                acc = t if acc is None else acc + t
            y_blocks.append(acc + y_off[:, jb * LANES:(jb + 1) * LANES])
        xdd_t = jnp.concatenate(
            [xdd_p[:, (g * bpg + jb) * LANES:(g * bpg + jb + 1) * LANES].T for jb in range(bpg)],
            axis=0).astype(mm)
        s_new = jnp.dot(xdd_t, b_g, preferred_element_type=F32)
        for r in range(hpg):
            h = g * hpg + r
            scal = jnp.exp(cs_t[h:h + 1, L - 1:L])
            rows = slice(h * SSM_HEADDIM, (h + 1) * SSM_HEADDIM)
            h_ref[rows, :] = h_ref[rows, :] * scal + s_new[r * SSM_HEADDIM:(r + 1) * SSM_HEADDIM, :]

    y = jnp.concatenate(y_blocks, axis=1) + dsk * xs
    y = y * _silu(z)
    ms = jnp.mean(y * y, axis=-1, keepdims=True)
    return y * lax.rsqrt(ms + 1e-5) * nw


def _ssd_kernel(*refs, L, nc, nb, zero_state, n_alias):
    xs_ref, bc_ref, z_ref, xn_ref, wdt_ref = refs[:5]
    refs = refs[5:]
    if not zero_state:
        pre_ref, h0_ref = refs[:2]
        refs = refs[2:]
    cw_ref, cb_ref, dtb_ref, alog_ref, dsk_ref, nw_ref, tri_ref, e_ref = refs[:8]
    y_ref, hout_ref, cst_ref, xp_ref, h_ref = refs[8 + n_alias:]
    c = pl.program_id(1)
    off = SUBLANES - (SSM_CONV - 1)

    dtr_all = jnp.dot(xn_ref[...], wdt_ref[...], preferred_element_type=F32)
    consts = (dtb_ref[...], alog_ref[...], dsk_ref[...], nw_ref[...], tri_ref[...], e_ref[...])
    xs_all = xs_ref[...].astype(F32)
    bc_all = bc_ref[...].astype(F32)
    z_all = z_ref[...].astype(F32)
    ys = []
    for bi in range(nb):
        rows = slice(bi * L, (bi + 1) * L)

        @pl.when(c == 0)
        def _():
            if zero_state:
                xp_ref[bi, off:SUBLANES, :] = jnp.zeros((SSM_CONV - 1, 2 * WIDTH), F32)
                h_ref[bi] = jnp.zeros((WIDTH, SSM_STATE), F32)
            else:
                xp_ref[bi, off:SUBLANES, :] = pre_ref[0, bi]
                h_ref[bi] = h0_ref[0, bi]

        if nc > 1:
            @pl.when(c > 0)
            def _():
                xp_ref[bi, off:SUBLANES, :] = xp_ref[bi, L + off:L + SUBLANES, :]

        xp_ref[bi, SUBLANES:SUBLANES + L, 0:WIDTH] = xs_all[rows, :]
        xp_ref[bi, SUBLANES:SUBLANES + L, WIDTH:2 * WIDTH] = bc_all[rows, :]

        conv = cb_ref[...] + cw_ref[0:1, :] * xp_ref[bi, off:off + L, :]
        for k in range(1, SSM_CONV):
            conv = conv + cw_ref[k:k + 1, :] * xp_ref[bi, off + k:off + k + L, :]
        ys.append(_ssd_core(_silu(conv), z_all[rows, :], dtr_all[rows, :], h_ref.at[bi], consts, L=L))

        @pl.when(c == nc - 1)
        def _():
            hout_ref[0, bi] = h_ref[bi]
            cst_ref[0, bi] = xp_ref[bi, L + off:L + SUBLANES, :]

    y_ref[...] = jnp.concatenate(ys, axis=0).astype(y_ref.dtype)


def ssd_branch(proj, xn, lp, consts, l, *, nbatch, t, conv_state=None, ssm_state=None, prev=None):
    L = SSM_CHUNK if t % SSM_CHUNK == 0 else t
    nc = t // L
    zero_state = conv_state is None
    nb = 4 if nc == 1 and nbatch % 4 == 0 else 1
    m = nbatch * t
    rl = nb * L

    def rows(b, c):
        return b * nc + c

    def col(cidx):
        return pl.BlockSpec((rl, WIDTH), lambda b, c: (rows(b, c), cidx))

    tri, e = consts
    in_specs = [col(COL_XS), col(COL_BC), col(COL_Z),
                pl.BlockSpec((rl, D_MODEL), lambda b, c: (rows(b, c), 0)),
                _const_spec((D_MODEL, LANES))]
    args = [proj, proj, proj, xn, lp["w_dt"]]
    if not zero_state:
        in_specs += [pl.BlockSpec((1, nb, SSM_CONV - 1, 2 * WIDTH), lambda b, c: (l, b, 0, 0)),
                     pl.BlockSpec((1, nb, WIDTH, SSM_STATE), lambda b, c: (l, b, 0, 0))]
        args += [conv_state, ssm_state]
    in_specs += [_const_spec((SSM_CONV, 2 * WIDTH)), _const_spec((1, 2 * WIDTH)),
                 _const_spec((1, LANES)), _const_spec((1, LANES)), _const_spec((1, WIDTH)),
                 _const_spec((1, WIDTH)), _const_spec((SSM_CHUNK, SSM_CHUNK)),
                 _const_spec((LANES, WIDTH))]
    args += [lp["ssm_conv_w"], lp["ssm_conv_b"], lp["dt_bias"], lp["a_log"], lp["d_skip"],
             lp["ssm_norm_w"], tri, e]
    h_shape, h_spec, h_prev = _stacked_out(l, nbatch, nb, (WIDTH, SSM_STATE), prev and prev[0])
    c_shape, c_spec, c_prev = _stacked_out(l, nbatch, nb, (SSM_CONV - 1, 2 * WIDTH), prev and prev[1])
    alias_in = h_prev + c_prev
    aliases = {len(args) + i: 1 + i for i in range(len(alias_in))}
    in_specs += [pl.BlockSpec(memory_space=pl.ANY)] * len(alias_in)
    y, h_out, c_out = pl.pallas_call(
        functools.partial(_ssd_kernel, L=L, nc=nc, nb=nb, zero_state=zero_state, n_alias=len(alias_in)),
        grid=(nbatch // nb, nc),
        in_specs=in_specs,
        out_specs=[pl.BlockSpec((rl, WIDTH), lambda b, c: (rows(b, c), 0)), h_spec, c_spec],
        out_shape=[jax.ShapeDtypeStruct((m, WIDTH), BF16), h_shape, c_shape],
        scratch_shapes=[pltpu.VMEM((nb, SUBLANES + L, 2 * WIDTH), F32),
                        pltpu.VMEM((nb, WIDTH, SSM_STATE), F32)],
        input_output_aliases=aliases,
        compiler_params=_cparams(("parallel", "arbitrary")),
        name="ssd_branch",
    )(*args, *alias_in)
    return y, (h_out, c_out)


def _dwconv(xp_ref, w_ref, b_ref, acc_ref, *, k, start, tb, row0):
    rbk = min(tb, 32)
    for j in range(WIDTH // LANES):
        lanes = slice(j * LANES, (j + 1) * LANES)
        for r in range(tb // rbk):
            r0 = r * rbk
            acc = w_ref[0:1, lanes] * xp_ref[start + r0:start + r0 + rbk, lanes]
            if b_ref is not None:
                acc = acc + b_ref[:, lanes]
            for i in range(1, k):
                acc = acc + w_ref[i:i + 1, lanes] * xp_ref[start + r0 + i:start + r0 + i + rbk, lanes]
            acc_ref[row0 + r0:row0 + r0 + rbk, lanes] = acc


def _conv_all(u, xp_ref, pre_ref, st_ref, w_ref, b_ref, acc_ref, *, k, p, tb, nc, nb):
    c = pl.program_id(1)
    for bi in range(nb):
        @pl.when(c == 0)
        def _():
            if pre_ref is None:
                xp_ref[bi, p - (k - 1):p, :] = jnp.zeros((k - 1, WIDTH), F32)
            else:
                xp_ref[bi, p - (k - 1):p, :] = pre_ref[0, bi]

        if nc > 1:
            @pl.when(c > 0)
            def _():
                xp_ref[bi, 0:p, :] = xp_ref[bi, tb:tb + p, :]

        xp_ref[bi, p:p + tb, :] = u[bi * tb:(bi + 1) * tb, :]
        _dwconv(xp_ref.at[bi], w_ref, b_ref, acc_ref, k=k, start=p - (k - 1), tb=tb, row0=bi * tb)

        @pl.when(c == nc - 1)
        def _():
            st_ref[0, bi] = xp_ref[bi, p + tb - (k - 1):p + tb, :]


def _cf_kernel(*refs, tb, nc, nb, p, zero_state, n_alias):
    a_ref, g_ref, gate_ref = refs[:3]
    refs = refs[3:]
    pre_ref = None
    if not zero_state:
        pre_ref, refs = refs[0], refs[1:]
    w_ref, b_ref, lnw_ref, lnb_ref = refs[:4]
    o_ref, st_ref, xp_ref, acc_ref = refs[4 + n_alias:]
    u = a_ref[...].astype(F32) * _sigmoid(g_ref[...].astype(F32))
    _conv_all(u, xp_ref, pre_ref, st_ref, w_ref, b_ref, acc_ref, k=CF_CONV, p=p, tb=tb, nc=nc, nb=nb)
    u = acc_ref[...]
    mu = jnp.mean(u, axis=-1, keepdims=True)
    uc = u - mu
    var = jnp.mean(uc * uc, axis=-1, keepdims=True)
    u = _silu(uc * lax.rsqrt(var + 1e-5) * lnw_ref[...] + lnb_ref[...])
    o_ref[...] = (u * _silu(gate_ref[...].astype(F32))).astype(o_ref.dtype)


def _sc_kernel(*refs, tb, nc, nb, p, zero_state, n_alias):
    gb_ref, gc_ref, sv_ref, gate_ref = refs[:4]
    refs = refs[4:]
    pre_ref = None
    if not zero_state:
        pre_ref, refs = refs[0], refs[1:]
    w_ref = refs[0]
    o_ref, st_ref, xp_ref, acc_ref = refs[1 + n_alias:]
    u = gc_ref[...].astype(F32) * sv_ref[...].astype(F32)
    _conv_all(u, xp_ref, pre_ref, st_ref, w_ref, None, acc_ref, k=SC_CONV, p=p, tb=tb, nc=nc, nb=nb)
    o_ref[...] = (gb_ref[...].astype(F32) * acc_ref[...]
                  * _silu(gate_ref[...].astype(F32))).astype(o_ref.dtype)


def conv_branch(kind, proj, lp, l, *, nbatch, t, state=None, prev=None):
    tb = min(t, 128)
    nc = t // tb
    nb = 8 if nc == 1 and nbatch % 8 == 0 else 1
    m = nbatch * t
    rl = nb * tb
    zero_state = state is None
    k = CF_CONV if kind == "cf" else SC_CONV
    p = -(-(k - 1) // SUBLANES) * SUBLANES

    def rows(b, c):
        return b * nc + c

    def col(cidx):
        return pl.BlockSpec((rl, WIDTH), lambda b, c: (rows(b, c), cidx))

    if kind == "cf":
        in_specs = [col(COL_CF_A), col(COL_CF_G), col(COL_CF_GATE)]
        args = [proj, proj, proj]
        w_specs = [_const_spec((k, WIDTH)), _const_spec((1, WIDTH)), _const_spec((1, WIDTH)),
                   _const_spec((1, WIDTH))]
        w_args = [lp["cf_conv_w"], lp["cf_conv_b"], lp["cf_ln_w"], lp["cf_ln_b"]]
        body = _cf_kernel
    else:
        in_specs = [col(COL_SC_B), col(COL_SC_C), col(COL_SC_V), col(COL_SC_GATE)]
        args = [proj, proj, proj, proj]
        w_specs = [_const_spec((k, WIDTH))]
        w_args = [lp["sc_conv_w"]]
        body = _sc_kernel
    if not zero_state:
        in_specs.append(pl.BlockSpec((1, nb, k - 1, WIDTH), lambda b, c: (l, b, 0, 0)))
        args.append(state)
    in_specs += w_specs
    args += w_args
    s_shape, s_spec, alias_in = _stacked_out(l, nbatch, nb, (k - 1, WIDTH), prev)
    aliases = {len(args) + i: 1 + i for i in range(len(alias_in))}
    in_specs += [pl.BlockSpec(memory_space=pl.ANY)] * len(alias_in)
    return pl.pallas_call(
        functools.partial(body, tb=tb, nc=nc, nb=nb, p=p, zero_state=zero_state, n_alias=len(alias_in)),
        grid=(nbatch // nb, nc),
        in_specs=in_specs,
        out_specs=[pl.BlockSpec((rl, WIDTH), lambda b, c: (rows(b, c), 0)), s_spec],
        out_shape=[jax.ShapeDtypeStruct((m, WIDTH), BF16), s_shape],
        scratch_shapes=[pltpu.VMEM((nb, p + tb, WIDTH), F32), pltpu.VMEM((rl, WIDTH), F32)],
        input_output_aliases=aliases,
        compiler_params=_cparams(("parallel", "arbitrary")),
        name=kind + "_branch",
    )(*args, *alias_in)


def _sink_column(sink_ref, g, rows_per_head):
    n = ATT_REP * rows_per_head
    rid = lax.broadcasted_iota(jnp.int32, (n, 1), 0)
    col = jnp.full((n, 1), sink_ref[g * ATT_REP + ATT_REP - 1], F32)
    for r in range(ATT_REP - 2, -1, -1):
        col = jnp.where(rid < (r + 1) * rows_per_head, sink_ref[g * ATT_REP + r], col)
    return col


def _softmax_sink(s, ok, sink_col):
    s = jnp.where(ok, s, -jnp.inf)
    m = jnp.maximum(jnp.max(s, axis=-1, keepdims=True), sink_col)
    p = jnp.exp(s - m)
    denom = jnp.sum(p, axis=-1, keepdims=True) + jnp.exp(sink_col - m)
    return p / denom


def _attn_prompt_kernel(q_ref, kp_ref, kc_ref, vp_ref, vc_ref, gate_ref, sink_ref, *rest, nblk, n_alias):
    o_ref, kw_ref, vw_ref = rest[n_alias:]
    L = WINDOW
    i = pl.program_id(1)
    q = q_ref[...].astype(F32) * (ATT_HEAD_DIM ** -0.5)
    k2 = jnp.concatenate([kp_ref[...], kc_ref[...]], axis=0).astype(F32)
    v2 = jnp.concatenate([vp_ref[...], vc_ref[...]], axis=0).astype(F32)
    li = lax.broadcasted_iota(jnp.int32, (ATT_REP * L, 2 * L), 0) & (L - 1)
    sj = lax.broadcasted_iota(jnp.int32, (ATT_REP * L, 2 * L), 1)
    ok = (sj > li) & (sj <= li + WINDOW) & ((sj >= L) | (i > 0))
    lane = lax.broadcasted_iota(jnp.int32, (2 * L, LANES), 1)
    lo_half = lane < ATT_HEAD_DIM

    o_blocks = [None] * (WIDTH // LANES)
    for g in range(ATT_KV_HEADS):
        half = g % 2
        sl = slice((g // 2) * LANES, (g // 2 + 1) * LANES)
        keep = lo_half if half == 0 else jnp.logical_not(lo_half)
        k_use = jnp.where(keep, k2[:, sl], 0.0).astype(BF16)
        v_use = jnp.where(keep, v2[:, sl], 0.0).astype(BF16)
        parts = []
        for r in range(ATT_REP):
            h = g * ATT_REP + r
            qb = q[:, (h // 2) * LANES:(h // 2 + 1) * LANES]
            parts.append(qb if h % 2 == half else pltpu.roll(qb, ATT_HEAD_DIM, 1))
        lhs = jnp.concatenate(parts, axis=0).astype(BF16)
        s = lax.dot_general(lhs, k_use, (((1,), (1,)), ((), ())), preferred_element_type=F32)
        pn = _softmax_sink(s, ok, _sink_column(sink_ref, g, L)).astype(BF16)
        o = jnp.dot(pn, v_use, preferred_element_type=F32)
        for r in range(ATT_REP):
            h = g * ATT_REP + r
            ob = o[r * L:(r + 1) * L, :]
            if h % 2 != half:
                ob = pltpu.roll(ob, ATT_HEAD_DIM, 1)
            o_blocks[h // 2] = ob if o_blocks[h // 2] is None else o_blocks[h // 2] + ob
    o = jnp.concatenate(o_blocks, axis=1)
    o_ref[...] = (o * _silu(gate_ref[...].astype(F32))).astype(o_ref.dtype)

    @pl.when(i == nblk - 1)
    def _():
        kc = kc_ref[...].astype(F32)
        vc = vc_ref[...].astype(F32)
        for g in range(ATT_KV_HEADS):
            kw_ref[0, 0, :, g, :] = kc[:, g * ATT_HEAD_DIM:(g + 1) * ATT_HEAD_DIM]
            vw_ref[0, 0, :, g, :] = vc[:, g * ATT_HEAD_DIM:(g + 1) * ATT_HEAD_DIM]


def _win_out(l, nbatch, nb, prev):
    return _stacked_out(l, nbatch, nb, (WINDOW, ATT_KV_HEADS, ATT_HEAD_DIM), prev)


def attn_prompt(proj, sinks, l, *, nbatch, t, prev=None):
    L = WINDOW
    nblk = t // L

    def col(cidx):
        return pl.BlockSpec((L, WIDTH), lambda b, i: (b * nblk + i, cidx))

    def kv(cidx, back):
        return pl.BlockSpec((L, ATT_KV_WIDTH), lambda b, i: (b * nblk + jnp.maximum(i - back, 0), cidx))

    k_shape, k_spec, k_prev = _win_out(l, nbatch, 1, prev and prev[0])
    v_shape, v_spec, v_prev = _win_out(l, nbatch, 1, prev and prev[1])
    alias_in = k_prev + v_prev
    args = [proj, proj, proj, proj, proj, proj, sinks]
    aliases = {len(args) + i: 1 + i for i in range(len(alias_in))}
    y, kw, vw = pl.pallas_call(
        functools.partial(_attn_prompt_kernel, nblk=nblk, n_alias=len(alias_in)),
        grid=(nbatch, nblk),
        in_specs=[col(COL_Q), kv(COL_K, 1), kv(COL_K, 0), kv(COL_V, 1), kv(COL_V, 0),
                  col(COL_ATT_GATE), pl.BlockSpec(memory_space=pltpu.SMEM)]
        + [pl.BlockSpec(memory_space=pl.ANY)] * len(alias_in),
        out_specs=[pl.BlockSpec((L, WIDTH), lambda b, i: (b * nblk + i, 0)), k_spec, v_spec],
        out_shape=[jax.ShapeDtypeStruct((nbatch * t, WIDTH), BF16), k_shape, v_shape],
        input_output_aliases=aliases,
        compiler_params=_cparams(("parallel", "arbitrary")),
        name="attn_prompt",
    )(*args, *alias_in)
    return y, (kw, vw)


def _attn_sample_kernel(q_ref, kn_ref, vn_ref, ck_ref, cv_ref, gate_ref, sink_ref, *rest, L, nb, n_alias):
    o_ref, kw_ref, vw_ref = rest[n_alias:]
    q = q_ref[...].astype(F32) * (ATT_HEAD_DIM ** -0.5)
    kn = kn_ref[...].astype(F32)
    vn = vn_ref[...].astype(F32)
    nkeys = 2 * WINDOW
    li = lax.broadcasted_iota(jnp.int32, (ATT_REP * L, nkeys), 0) & (L - 1)
    sj = lax.broadcasted_iota(jnp.int32, (ATT_REP * L, nkeys), 1)
    ok = (sj > li) & (sj <= li + WINDOW)
    sink_cols = [_sink_column(sink_ref, g, L) for g in range(ATT_KV_HEADS)]
    pad = jnp.zeros((nkeys - WINDOW - L, ATT_HEAD_DIM), F32)
    outs = []
    for bi in range(nb):
        rows = slice(bi * L, (bi + 1) * L)
        pieces = [None] * ATT_HEADS
        for g in range(ATT_KV_HEADS):
            gl = slice(g * ATT_HEAD_DIM, (g + 1) * ATT_HEAD_DIM)
            k2 = jnp.concatenate([ck_ref[0, bi, :, g, :], kn[rows, gl], pad], axis=0)
            v2 = jnp.concatenate([cv_ref[0, bi, :, g, :], vn[rows, gl], pad], axis=0)
            lhs = jnp.concatenate(
                [q[rows, (g * ATT_REP + r) * ATT_HEAD_DIM:(g * ATT_REP + r + 1) * ATT_HEAD_DIM]
                 for r in range(ATT_REP)], axis=0)
            s = lax.dot_general(lhs.astype(BF16), k2.astype(BF16), (((1,), (1,)), ((), ())),
                                preferred_element_type=F32)
            pn = _softmax_sink(s, ok, sink_cols[g]).astype(BF16)
            o = jnp.dot(pn, v2.astype(BF16), preferred_element_type=F32)
            for r in range(ATT_REP):
                pieces[g * ATT_REP + r] = o[r * L:(r + 1) * L, :]
            kw_ref[0, bi, :, g, :] = k2[L:L + WINDOW, :]
            vw_ref[0, bi, :, g, :] = v2[L:L + WINDOW, :]
        outs.append(jnp.concatenate(pieces, axis=1))
    o = jnp.concatenate(outs, axis=0)
    o_ref[...] = (o * _silu(gate_ref[...].astype(F32))).astype(o_ref.dtype)


def attn_sample(proj, sinks, cache_k, cache_v, l, *, nbatch, t, prev=None):
    nb = 4 if nbatch % 4 == 0 else 1
    rl = nb * t

    def col(cidx, w):
        return pl.BlockSpec((rl, w), lambda b: (b, cidx))

    cache = pl.BlockSpec((1, nb, WINDOW, ATT_KV_HEADS, ATT_HEAD_DIM), lambda b: (l, b, 0, 0, 0))
    k_shape, k_spec, k_prev = _win_out(l, nbatch, nb, prev and prev[0])
    v_shape, v_spec, v_prev = _win_out(l, nbatch, nb, prev and prev[1])
    alias_in = k_prev + v_prev
    args = [proj, proj, proj, cache_k, cache_v, proj, sinks]
    aliases = {len(args) + i: 1 + i for i in range(len(alias_in))}
    y, kw, vw = pl.pallas_call(
        functools.partial(_attn_sample_kernel, L=t, nb=nb, n_alias=len(alias_in)),
        grid=(nbatch // nb,),
        in_specs=[col(COL_Q, WIDTH), col(COL_K, ATT_KV_WIDTH), col(COL_V, ATT_KV_WIDTH), cache, cache,
                  col(COL_ATT_GATE, WIDTH), pl.BlockSpec(memory_space=pltpu.SMEM)]
        + [pl.BlockSpec(memory_space=pl.ANY)] * len(alias_in),
        out_specs=[pl.BlockSpec((rl, WIDTH), lambda b: (b, 0)), k_spec, v_spec],
        out_shape=[jax.ShapeDtypeStruct((nbatch * t, WIDTH), BF16), k_shape, v_shape],
        input_output_aliases=aliases,
        compiler_params=_cparams(("parallel",)),
        name="attn_sample",
    )(*args, *alias_in)
    return y, (kw, vw)


def _prep_layer(l, norm_w, w_in, ssm_conv_w, ssm_conv_b, ssm_dt_bias, ssm_a_log, ssm_d, ssm_norm_w,
                w_out_ssm, cf_conv_w, cf_conv_b, cf_ln_w, cf_ln_b, w_out_cf, sc_conv_w, w_out_sc,
                att_sinks, w_out_att, w_o):
    w = w_in[l]
    o = 0
    seg = {}
    for name, width in (("z", WIDTH), ("xbc", 2 * WIDTH), ("dt", SSM_HEADS), ("cf_in", 2 * WIDTH),
                        ("cf_gate", WIDTH), ("sc_in", 3 * WIDTH), ("sc_gate", WIDTH), ("q", WIDTH),
                        ("k", ATT_KV_WIDTH), ("v", ATT_KV_WIDTH), ("att_gate", WIDTH),
                        ("merge", N_BRANCH * D_MODEL)):
        seg[name] = w[:, o:o + width]
        o += width
    w_proj = jnp.concatenate([seg["merge"], seg["z"], seg["xbc"], seg["cf_in"], seg["cf_gate"],
                              seg["sc_in"], seg["sc_gate"], seg["q"], seg["att_gate"], seg["k"],
                              seg["v"]], axis=1).astype(BF16)
    w_dt = jnp.pad(seg["dt"], ((0, 0), (0, LANES - SSM_HEADS))).astype(BF16)
    pad_h = (0, LANES - SSM_HEADS)
    return dict(
        norm_w=norm_w[l], w_proj=w_proj, w_dt=w_dt,
        ssm_conv_w=ssm_conv_w[l], ssm_conv_b=ssm_conv_b[l].reshape(1, -1),
        dt_bias=jnp.pad(ssm_dt_bias[l], pad_h).reshape(1, LANES),
        a_log=jnp.pad(ssm_a_log[l], pad_h).reshape(1, LANES),
        d_skip=jnp.repeat(ssm_d[l], SSM_HEADDIM).reshape(1, WIDTH),
        ssm_norm_w=ssm_norm_w[l].reshape(1, WIDTH),
        cf_conv_w=cf_conv_w[l], cf_conv_b=cf_conv_b[l].reshape(1, WIDTH),
        cf_ln_w=cf_ln_w[l].reshape(1, WIDTH), cf_ln_b=cf_ln_b[l].reshape(1, WIDTH),
        sc_conv_w=sc_conv_w[l], sinks=att_sinks[l],
        w_out=jnp.stack([w_out_ssm[l], w_out_cf[l], w_out_sc[l], w_out_att[l]]).astype(BF16),
        w_o=w_o[l].astype(BF16),
    )


def _layer(x, lp, consts, l, states, prev, *, nbatch, t):
    kw = dict(nbatch=nbatch, t=t)
    xn = rmsnorm(x, lp["norm_w"], EPS, BF16)
    proj = matmul(xn, lp["w_proj"], BF16, tm=2048, tn=512, name="in_proj")
    if states is None:
        ya, st_a = ssd_branch(proj, xn, lp, consts, l, prev=prev and prev[0], **kw)
        yb, st_b = conv_branch("cf", proj, lp, l, prev=prev and prev[1], **kw)
        yc, st_c = conv_branch("sc", proj, lp, l, prev=prev and prev[2], **kw)
        yd, st_d = attn_prompt(proj, lp["sinks"], l, prev=prev and prev[3], **kw)
    else:
        ssm, conv_ssm, conv_cf, conv_sc, cache_k, cache_v = states
        ya, st_a = ssd_branch(proj, xn, lp, consts, l, conv_state=conv_ssm, ssm_state=ssm,
                              prev=prev and prev[0], **kw)
        yb, st_b = conv_branch("cf", proj, lp, l, state=conv_cf, prev=prev and prev[1], **kw)
        yc, st_c = conv_branch("sc", proj, lp, l, state=conv_sc, prev=prev and prev[2], **kw)
        yd, st_d = attn_sample(proj, lp["sinks"], cache_k, cache_v, l, prev=prev and prev[3], **kw)
    hmix = merge_branches((ya, yb, yc, yd), lp["w_out"], proj, tm=1024, tn=512)
    x = matmul_residual(hmix, lp["w_o"], x, tm=1024, tn=1024)
    return x, (st_a, st_b, st_c, st_d)


def kernel(x_prompt, x_sample, state_ssm, state_conv_ssm, state_conv_cf, state_conv_sc, cache_k, cache_v, norm_w, w_in, ssm_conv_w, ssm_conv_b, ssm_dt_bias, ssm_a_log, ssm_d, ssm_norm_w, w_out_ssm, cf_conv_w, cf_conv_b, cf_ln_w, cf_ln_b, w_out_cf, sc_conv_w, w_out_sc, att_sinks, w_out_att, w_o, final_norm_w):
    bp, tp, _ = x_prompt.shape
    bs, ts, _ = x_sample.shape
    xp = x_prompt.reshape(bp * tp, D_MODEL)
    xs = x_sample.reshape(bs * ts, D_MODEL)

    tri = jnp.tril(jnp.ones((SSM_CHUNK, SSM_CHUNK), F32))
    expand = (jnp.arange(WIDTH)[None, :] // SSM_HEADDIM == jnp.arange(LANES)[:, None]).astype(F32)
    consts = (tri, expand)
    s_states = (state_ssm.reshape(DEPTH, bs, WIDTH, SSM_STATE), state_conv_ssm, state_conv_cf,
                state_conv_sc, cache_k, cache_v)

    p_prev = s_prev = None
    for l in range(DEPTH):
        lp = _prep_layer(l, norm_w, w_in, ssm_conv_w, ssm_conv_b, ssm_dt_bias, ssm_a_log, ssm_d,
                         ssm_norm_w, w_out_ssm, cf_conv_w, cf_conv_b, cf_ln_w, cf_ln_b, w_out_cf,
                         sc_conv_w, w_out_sc, att_sinks, w_out_att, w_o)
        xp, p_prev = _layer(xp, lp, consts, l, None, p_prev, nbatch=bp, t=tp)
        xs, s_prev = _layer(xs, lp, consts, l, s_states, s_prev, nbatch=bs, t=ts)

    y_prompt = rmsnorm(xp, final_norm_w, EPS, F32).reshape(bp, tp, D_MODEL)
    y_sample = rmsnorm(xs, final_norm_w, EPS, F32).reshape(bs, ts, D_MODEL)

    def unpack(st, nbatch):
        (h, conv_ssm), conv_cf, conv_sc, (kwin, vwin) = st
        return (h.reshape(DEPTH, nbatch, SSM_HEADS, SSM_HEADDIM, SSM_STATE), conv_ssm, conv_cf,
                conv_sc, kwin, vwin)

    return (y_prompt, y_sample, *unpack(p_prev, bp), *unpack(s_prev, bs))
```

```python
import functools

import jax
import jax.numpy as jnp
from jax import lax
from jax.experimental import pallas as pl
from jax.experimental.pallas import tpu as pltpu

F32 = jnp.float32
BF16 = jnp.bfloat16

D_MODEL = 2048
DEPTH = 2
N_BRANCH = 4
WIDTH = 1024
SSM_HEADS = 16
SSM_HEADDIM = 64
SSM_GROUPS = 4
SSM_STATE = 128
SSM_CONV = 4
SSM_CHUNK = 128
CF_CONV = 31
SC_CONV = 3
ATT_HEADS = 16
ATT_KV_HEADS = 4
ATT_HEAD_DIM = 64
ATT_KV_WIDTH = ATT_KV_HEADS * ATT_HEAD_DIM
ATT_REP = ATT_HEADS // ATT_KV_HEADS
WINDOW = 128
EPS = 1e-6

LANES = 128
SUBLANES = 8
VMEM_LIMIT = 56 * 1024 * 1024

COL_MERGE = 0
COL_Z = 8
COL_XS = 9
COL_BC = 10
COL_CF_A = 11
COL_CF_G = 12
COL_CF_GATE = 13
COL_SC_B = 14
COL_SC_C = 15
COL_SC_V = 16
COL_SC_GATE = 17
COL_Q = 18
COL_ATT_GATE = 19
COL_K = 80
COL_V = 81
PROJ_COLS = 20992


def _sigmoid(x):
    return 1.0 / (1.0 + jnp.exp(-x))


def _silu(x):
    return x * _sigmoid(x)


def _cparams(sem, limit=VMEM_LIMIT):
    return pltpu.CompilerParams(dimension_semantics=sem, vmem_limit_bytes=limit)


def _tile(m, target):
    t = min(m, target)
    while m % t or t % SUBLANES:
        t -= 1
    return t


def _const_spec(shape):
    return pl.BlockSpec(shape, lambda *_: (0,) * len(shape))


def _stacked_out(l, nbatch, nb, tail, prev):
    shape = (DEPTH, nbatch) + tail
    spec = pl.BlockSpec((1, nb) + tail, lambda b, *_: (l, b) + (0,) * len(tail))
    return jax.ShapeDtypeStruct(shape, F32), spec, ([] if prev is None else [prev])


def _rmsnorm_kernel(x_ref, w_ref, o_ref, *, eps):
    x = x_ref[...].astype(F32)
    ms = jnp.mean(x * x, axis=-1, keepdims=True)
    o_ref[...] = (x * lax.rsqrt(ms + eps) * w_ref[...]).astype(o_ref.dtype)


def rmsnorm(x, w, eps, out_dtype):
    m, d = x.shape
    tm = _tile(m, 512)
    return pl.pallas_call(
        functools.partial(_rmsnorm_kernel, eps=eps),
        grid=(m // tm,),
        in_specs=[pl.BlockSpec((tm, d), lambda i: (i, 0)), _const_spec((1, d))],
        out_specs=pl.BlockSpec((tm, d), lambda i: (i, 0)),
        out_shape=jax.ShapeDtypeStruct((m, d), out_dtype),
        compiler_params=_cparams(("parallel",)),
        name="rmsnorm",
    )(x, w.reshape(1, d))


_SEG_START = dict(z=0, xs=1024, bc=2048, dt=3072, cf_a=3088, cf_g=4112, cf_gate=5136, sc_b=6160,
                  sc_c=7184, sc_v=8208, sc_gate=9232, q=10256, k=11280, att_gate=11792, merge=12816)
PROJ_TN = 512


def _proj_block_starts():
    starts = [_SEG_START["merge"] + PROJ_TN * j for j in range(N_BRANCH * D_MODEL // PROJ_TN)]
    for name in ("z", "xs", "bc", "cf_a", "cf_g", "cf_gate", "sc_b", "sc_c", "sc_v", "sc_gate", "q",
                 "att_gate"):
        starts += [_SEG_START[name], _SEG_START[name] + PROJ_TN]
    starts.append(_SEG_START["k"])
    assert len(starts) * PROJ_TN == PROJ_COLS and all(v % SUBLANES == 0 for v in starts)
    return jnp.array([v // SUBLANES for v in starts], jnp.int32)


def _inproj_kernel(starts_ref, a_ref, w_ref, o_ref):
    del starts_ref
    w = w_ref[0].astype(BF16)
    o_ref[...] = lax.dot_general(a_ref[...], w, (((1,), (1,)), ((), ())),
                                 preferred_element_type=F32).astype(o_ref.dtype)


def in_proj(xn, w_t, l, tm):
    m, k = xn.shape
    tm = _tile(m, tm)
    grid_spec = pltpu.PrefetchScalarGridSpec(
        num_scalar_prefetch=1, grid=(m // tm, PROJ_COLS // PROJ_TN),
        in_specs=[pl.BlockSpec((tm, k), lambda i, j, s: (i, 0)),
                  pl.BlockSpec((pl.Element(1), pl.Element(PROJ_TN), pl.Element(k)),
                               lambda i, j, s: (l, s[j] * SUBLANES, 0))],
        out_specs=pl.BlockSpec((tm, PROJ_TN), lambda i, j, s: (i, j)))
    return pl.pallas_call(
        _inproj_kernel, grid_spec=grid_spec,
        out_shape=jax.ShapeDtypeStruct((m, PROJ_COLS), BF16),
        compiler_params=_cparams(("parallel", "arbitrary")),
        name="in_proj",
    )(_proj_block_starts(), xn, w_t)


def _out_kernel(h_ref, w_ref, x_ref, nw_ref, *out_refs, eps, last):
    x = x_ref[...] + jnp.dot(h_ref[...], w_ref[...], preferred_element_type=F32)
    ms = jnp.mean(x * x, axis=-1, keepdims=True)
    xn = x * lax.rsqrt(ms + eps) * nw_ref[...]
    if last:
        out_refs[0][...] = xn
    else:
        out_refs[0][...] = x
        out_refs[1][...] = xn.astype(out_refs[1].dtype)


def out_proj_norm(h, w, x, norm_w, eps, last, tm):
    m, k = h.shape
    n = w.shape[1]
    tm = _tile(m, tm)
    row = lambda i: (i, 0)
    if last:
        out_shape = [jax.ShapeDtypeStruct((m, n), F32)]
    else:
        out_shape = [jax.ShapeDtypeStruct((m, n), F32), jax.ShapeDtypeStruct((m, n), BF16)]
    return pl.pallas_call(
        functools.partial(_out_kernel, eps=eps, last=last),
        grid=(m // tm,),
        in_specs=[pl.BlockSpec((tm, k), row),
                  pl.BlockSpec((k, n), lambda i: (0, 0), pipeline_mode=pl.Buffered(1)),
                  pl.BlockSpec((tm, n), row), _const_spec((1, n))],
        out_specs=[pl.BlockSpec((tm, n), row)] * len(out_shape),
        out_shape=out_shape,
        compiler_params=_cparams(("parallel",)),
        name="out_proj_norm",
    )(h, w, x, norm_w.reshape(1, n))


def _merge_kernel(ba_ref, bb_ref, bc_ref, bd_ref, w_ref, m0_ref, m1_ref, m2_ref, m3_ref, o_ref):
    acc = None
    for i, (b_ref, m_ref) in enumerate(((ba_ref, m0_ref), (bb_ref, m1_ref),
                                        (bc_ref, m2_ref), (bd_ref, m3_ref))):
        out = jnp.dot(b_ref[...], w_ref[i], preferred_element_type=F32)
        term = _sigmoid(m_ref[...].astype(F32)) * out
        acc = term if acc is None else acc + term
    o_ref[...] = acc.astype(o_ref.dtype)


def merge_branches(branches, w_out, proj, tm, tn):
    m = proj.shape[0]
    tm = _tile(m, tm)
    nb = D_MODEL // tn
    br_spec = pl.BlockSpec((tm, WIDTH), lambda i, j: (i, 0))
    merge_specs = [pl.BlockSpec((tm, tn), functools.partial(lambda i, j, b: (i, b * nb + j), b=b))
                   for b in range(N_BRANCH)]
    return pl.pallas_call(
        _merge_kernel,
        grid=(m // tm, nb),
        in_specs=[br_spec] * 4 + [pl.BlockSpec((N_BRANCH, WIDTH, tn), lambda i, j: (0, 0, j))] + merge_specs,
        out_specs=pl.BlockSpec((tm, tn), lambda i, j: (i, j)),
        out_shape=jax.ShapeDtypeStruct((m, D_MODEL), BF16),
        compiler_params=_cparams(("parallel", "arbitrary")),
        name="merge_branches",
    )(*branches, w_out, proj, proj, proj, proj)


def _hdot(a, b):
    return jnp.dot(a, b, precision=lax.Precision.HIGHEST, preferred_element_type=F32)


def _pad_rows(x, rows):
    if x.shape[0] == rows:
        return x
    return jnp.concatenate([x, jnp.zeros((rows - x.shape[0], x.shape[1]), x.dtype)], axis=0)


def _ssd_core(xbc, z, dtr, h_ref, consts, *, L):
    dtb, alog, dsk, nw, tri, e = consts
    mm = BF16 if L >= 16 else F32
    xs = xbc[:, 0:WIDTH]
    bmat = xbc[:, WIDTH:WIDTH + SSM_GROUPS * SSM_STATE]
    cmat = xbc[:, WIDTH + SSM_GROUPS * SSM_STATE:]

    dtr = dtr + dtb
    dt = jnp.maximum(dtr, 0.0) + jnp.log1p(jnp.exp(-jnp.abs(dtr)))
    da = dt * (-jnp.exp(alog))
    if L == SSM_CHUNK:
        cs = _hdot(tri, da)
    else:
        row = lax.broadcasted_iota(jnp.int32, (L, LANES), 0)
        cs = jnp.zeros((L, LANES), F32)
        for s in range(L):
            cs = cs + jnp.where(row >= s, da[s:s + 1, :], 0.0)
    cs_last = cs[L - 1:L, :]
    ecs = jnp.exp(cs)
    dec = jnp.exp(cs_last - cs)
    dt_e = _hdot(dt, e)
    ecs_e = _hdot(ecs, e)
    dec_e = _hdot(dec, e)
    xdt = xs * dt_e
    xdd = xdt * dec_e

    lp = SSM_CHUNK
    xdt_p = _pad_rows(xdt, lp).astype(mm)
    xdd_p = _pad_rows(xdd, lp)
    b_p = _pad_rows(bmat, lp).astype(mm)
    cs_t = _pad_rows(cs, lp).T
    cmat_m = cmat.astype(mm)

    li = lax.broadcasted_iota(jnp.int32, (L, lp), 0)
    sj = lax.broadcasted_iota(jnp.int32, (L, lp), 1)
    causal = sj <= li
    lane = lax.broadcasted_iota(jnp.int32, (lp, LANES), 1)
    lo_half = lane < SSM_HEADDIM

    y_blocks = []
    hpg = SSM_HEADS // SSM_GROUPS
    gw = hpg * SSM_HEADDIM
    bpg = gw // LANES
    for g in range(SSM_GROUPS):
        c_g = cmat_m[:, g * SSM_STATE:(g + 1) * SSM_STATE]
        b_g = b_p[:, g * SSM_STATE:(g + 1) * SSM_STATE]
        cb = lax.dot_general(c_g, b_g, (((1,), (1,)), ((), ())), preferred_element_type=F32)
        h_blk = h_ref[g * gw:(g + 1) * gw, :]
        y_off = lax.dot_general(c_g, h_blk.astype(mm), (((1,), (1,)), ((), ())),
                                preferred_element_type=F32)
        y_off = y_off * ecs_e[:, g * gw:(g + 1) * gw]
        for jb in range(bpg):
            blk = g * bpg + jb
            x_blk = xdt_p[:, blk * LANES:(blk + 1) * LANES]
            acc = None
            for par in range(2):
                h = 2 * blk + par
                diff = cs[:, h:h + 1] - cs_t[h:h + 1, :]
                lm = jnp.exp(jnp.where(causal, diff, -jnp.inf))
                gmat = (cb * lm).astype(mm)
                x_half = jnp.where(lo_half if par == 0 else jnp.logical_not(lo_half), x_blk,
                                   jnp.zeros_like(x_blk))
                t = jnp.dot(gmat, x_half, preferred_element_type=F32)
                acc = t if acc is None else acc + t
            y_blocks.append(acc + y_off[:, jb * LANES:(jb + 1) * LANES])
        xdd_t = jnp.concatenate(
            [xdd_p[:, (g * bpg + jb) * LANES:(g * bpg + jb + 1) * LANES].T for jb in range(bpg)],
            axis=0).astype(mm)
        s_new = jnp.dot(xdd_t, b_g, preferred_element_type=F32)
        for r in range(hpg):
            h = g * hpg + r
            scal = jnp.exp(cs_t[h:h + 1, L - 1:L])
            rows = slice(h * SSM_HEADDIM, (h + 1) * SSM_HEADDIM)
            h_ref[rows, :] = h_ref[rows, :] * scal + s_new[r * SSM_HEADDIM:(r + 1) * SSM_HEADDIM, :]

    y = jnp.concatenate(y_blocks, axis=1) + dsk * xs
    y = y * _silu(z)
    ms = jnp.mean(y * y, axis=-1, keepdims=True)
    return y * lax.rsqrt(ms + 1e-5) * nw


def _ssd_kernel(*refs, L, nc, nb, zero_state, n_alias):
    xs_ref, bc_ref, z_ref, xn_ref, wdt_ref = refs[:5]
    refs = refs[5:]
    if not zero_state:
        pre_ref, h0_ref = refs[:2]
        refs = refs[2:]
    cw_ref, cb_ref, dtb_ref, alog_ref, dsk_ref, nw_ref, tri_ref, e_ref = refs[:8]
    y_ref, hout_ref, cst_ref, xp_ref, h_ref = refs[8 + n_alias:]
    c = pl.program_id(1)
    off = SUBLANES - (SSM_CONV - 1)

    dtr_all = lax.dot_general(xn_ref[...], wdt_ref[0].astype(BF16), (((1,), (1,)), ((), ())),
                              preferred_element_type=F32)
    consts = (dtb_ref[...], alog_ref[...], dsk_ref[...], nw_ref[...], tri_ref[...], e_ref[...])
    xs_all = xs_ref[...].astype(F32)
    bc_all = bc_ref[...].astype(F32)
    z_all = z_ref[...].astype(F32)
    ys = []
    for bi in range(nb):
        rows = slice(bi * L, (bi + 1) * L)

        @pl.when(c == 0)
        def _():
            if zero_state:
                xp_ref[bi, off:SUBLANES, :] = jnp.zeros((SSM_CONV - 1, 2 * WIDTH), F32)
                h_ref[bi] = jnp.zeros((WIDTH, SSM_STATE), F32)
            else:
                xp_ref[bi, off:SUBLANES, :] = pre_ref[0, bi]
                h_ref[bi] = h0_ref[0, bi]

        if nc > 1:
            @pl.when(c > 0)
            def _():
                xp_ref[bi, off:SUBLANES, :] = xp_ref[bi, L + off:L + SUBLANES, :]

        xp_ref[bi, SUBLANES:SUBLANES + L, 0:WIDTH] = xs_all[rows, :]
        xp_ref[bi, SUBLANES:SUBLANES + L, WIDTH:2 * WIDTH] = bc_all[rows, :]

        conv = cb_ref[...] + cw_ref[0:1, :] * xp_ref[bi, off:off + L, :]
        for k in range(1, SSM_CONV):
            conv = conv + cw_ref[k:k + 1, :] * xp_ref[bi, off + k:off + k + L, :]
        ys.append(_ssd_core(_silu(conv), z_all[rows, :], dtr_all[rows, :], h_ref.at[bi], consts, L=L))

        @pl.when(c == nc - 1)
        def _():
            hout_ref[0, bi] = h_ref[bi]
            cst_ref[0, bi] = xp_ref[bi, L + off:L + SUBLANES, :]

    y_ref[...] = jnp.concatenate(ys, axis=0).astype(y_ref.dtype)


def ssd_branch(proj, xn, w_t, lp, consts, l, *, nbatch, t, conv_state=None, ssm_state=None, prev=None):
    L = SSM_CHUNK if t % SSM_CHUNK == 0 else t
    nc = t // L
    zero_state = conv_state is None
    nb = 4 if nc == 1 and nbatch % 4 == 0 else 1
    m = nbatch * t
    rl = nb * L

    def rows(b, c):
        return b * nc + c

    def col(cidx):
        return pl.BlockSpec((rl, WIDTH), lambda b, c: (rows(b, c), cidx))

    tri, e = consts
    in_specs = [col(COL_XS), col(COL_BC), col(COL_Z),
                pl.BlockSpec((rl, D_MODEL), lambda b, c: (rows(b, c), 0)),
                pl.BlockSpec((1, LANES, D_MODEL), lambda b, c: (l, _SEG_START["dt"] // LANES, 0))]
    args = [proj, proj, proj, xn, w_t]
    if not zero_state:
        in_specs += [pl.BlockSpec((1, nb, SSM_CONV - 1, 2 * WIDTH), lambda b, c: (l, b, 0, 0)),
                     pl.BlockSpec((1, nb, WIDTH, SSM_STATE), lambda b, c: (l, b, 0, 0))]
        args += [conv_state, ssm_state]
    in_specs += [_const_spec((SSM_CONV, 2 * WIDTH)), _const_spec((1, 2 * WIDTH)),
                 _const_spec((1, LANES)), _const_spec((1, LANES)), _const_spec((1, WIDTH)),
                 _const_spec((1, WIDTH)), _const_spec((SSM_CHUNK, SSM_CHUNK)),
                 _const_spec((LANES, WIDTH))]
    args += [lp["ssm_conv_w"], lp["ssm_conv_b"], lp["dt_bias"], lp["a_log"], lp["d_skip"],
             lp["ssm_norm_w"], tri, e]
    h_shape, h_spec, h_prev = _stacked_out(l, nbatch, nb, (WIDTH, SSM_STATE), prev and prev[0])
    c_shape, c_spec, c_prev = _stacked_out(l, nbatch, nb, (SSM_CONV - 1, 2 * WIDTH), prev and prev[1])
    alias_in = h_prev + c_prev
    aliases = {len(args) + i: 1 + i for i in range(len(alias_in))}
    in_specs += [pl.BlockSpec(memory_space=pl.ANY)] * len(alias_in)
    y, h_out, c_out = pl.pallas_call(
        functools.partial(_ssd_kernel, L=L, nc=nc, nb=nb, zero_state=zero_state, n_alias=len(alias_in)),
        grid=(nbatch // nb, nc),
        in_specs=in_specs,
        out_specs=[pl.BlockSpec((rl, WIDTH), lambda b, c: (rows(b, c), 0)), h_spec, c_spec],
        out_shape=[jax.ShapeDtypeStruct((m, WIDTH), BF16), h_shape, c_shape],
        scratch_shapes=[pltpu.VMEM((nb, SUBLANES + L, 2 * WIDTH), F32),
                        pltpu.VMEM((nb, WIDTH, SSM_STATE), F32)],
        input_output_aliases=aliases,
        compiler_params=_cparams(("parallel", "arbitrary")),
        name="ssd_branch",
    )(*args, *alias_in)
    return y, (h_out, c_out)


def _dwconv(xp_ref, xsh_ref, w_ref, b_ref, acc_ref, *, k, start, tb, row0):
    rbk = min(tb, 32)
    if xsh_ref is not None:
        nrows = xsh_ref.shape[1]
        for s in range(1, SUBLANES):
            xsh_ref[s - 1] = xp_ref[s:s + nrows, :]
    for j in range(WIDTH // LANES):
        lanes = slice(j * LANES, (j + 1) * LANES)
        for r in range(tb // rbk):
            r0 = r * rbk
            acc = None
            for i in range(k):
                first = start + i + r0
                s = first % SUBLANES
                if xsh_ref is None or s == 0:
                    x = xp_ref[first:first + rbk, lanes]
                else:
                    x = xsh_ref[s - 1, first - s:first - s + rbk, lanes]
                term = w_ref[i:i + 1, lanes] * x
                if acc is None:
                    acc = term if b_ref is None else term + b_ref[:, lanes]
                else:
                    acc = acc + term
            acc_ref[row0 + r0:row0 + r0 + rbk, lanes] = acc


def _conv_all(u, xp_ref, pre_ref, st_ref, w_ref, b_ref, acc_ref, xsh_ref=None, *, k, p, tb, nc, nb):
    c = pl.program_id(1)
    for bi in range(nb):
        @pl.when(c == 0)
        def _():
            if pre_ref is None:
                xp_ref[bi, p - (k - 1):p, :] = jnp.zeros((k - 1, WIDTH), F32)
            else:
                xp_ref[bi, p - (k - 1):p, :] = pre_ref[0, bi]

        if nc > 1:
            @pl.when(c > 0)
            def _():
                xp_ref[bi, 0:p, :] = xp_ref[bi, tb:tb + p, :]

        xp_ref[bi, p:p + tb, :] = u[bi * tb:(bi + 1) * tb, :]
        _dwconv(xp_ref.at[bi], xsh_ref, w_ref, b_ref, acc_ref, k=k, start=p - (k - 1), tb=tb,
                row0=bi * tb)

        @pl.when(c == nc - 1)
        def _():
            st_ref[0, bi] = xp_ref[bi, p + tb - (k - 1):p + tb, :]


def _cf_kernel(*refs, tb, nc, nb, p, zero_state, n_alias):
    a_ref, g_ref, gate_ref = refs[:3]
    refs = refs[3:]
    pre_ref = None
    if not zero_state:
        pre_ref, refs = refs[0], refs[1:]
    w_ref, b_ref, lnw_ref, lnb_ref = refs[:4]
    o_ref, st_ref, xp_ref, acc_ref, *xsh = refs[4 + n_alias:]
    u = a_ref[...].astype(F32) * _sigmoid(g_ref[...].astype(F32))
    _conv_all(u, xp_ref, pre_ref, st_ref, w_ref, b_ref, acc_ref, *xsh, k=CF_CONV, p=p, tb=tb, nc=nc,
              nb=nb)
    u = acc_ref[...]
    mu = jnp.mean(u, axis=-1, keepdims=True)
    uc = u - mu
    var = jnp.mean(uc * uc, axis=-1, keepdims=True)
    u = _silu(uc * lax.rsqrt(var + 1e-5) * lnw_ref[...] + lnb_ref[...])
    o_ref[...] = (u * _silu(gate_ref[...].astype(F32))).astype(o_ref.dtype)


def _sc_kernel(*refs, tb, nc, nb, p, zero_state, n_alias):
    gb_ref, gc_ref, sv_ref, gate_ref = refs[:4]
    refs = refs[4:]
    pre_ref = None
    if not zero_state:
        pre_ref, refs = refs[0], refs[1:]
    w_ref = refs[0]
    o_ref, st_ref, xp_ref, acc_ref = refs[1 + n_alias:]
    u = gc_ref[...].astype(F32) * sv_ref[...].astype(F32)
    _conv_all(u, xp_ref, pre_ref, st_ref, w_ref, None, acc_ref, k=SC_CONV, p=p, tb=tb, nc=nc, nb=nb)
    o_ref[...] = (gb_ref[...].astype(F32) * acc_ref[...]
                  * _silu(gate_ref[...].astype(F32))).astype(o_ref.dtype)


def conv_branch(kind, proj, lp, l, *, nbatch, t, state=None, prev=None):
    tb = min(t, 128)
    nc = t // tb
    nb = 8 if nc == 1 and nbatch % 8 == 0 else 1
    m = nbatch * t
    rl = nb * tb
    zero_state = state is None
    k = CF_CONV if kind == "cf" else SC_CONV
    p = -(-(k - 1) // SUBLANES) * SUBLANES

    def rows(b, c):
        return b * nc + c

    def col(cidx):
        return pl.BlockSpec((rl, WIDTH), lambda b, c: (rows(b, c), cidx))

    if kind == "cf":
        in_specs = [col(COL_CF_A), col(COL_CF_G), col(COL_CF_GATE)]
        args = [proj, proj, proj]
        w_specs = [_const_spec((k, WIDTH)), _const_spec((1, WIDTH)), _const_spec((1, WIDTH)),
                   _const_spec((1, WIDTH))]
        w_args = [lp["cf_conv_w"], lp["cf_conv_b"], lp["cf_ln_w"], lp["cf_ln_b"]]
        body = _cf_kernel
    else:
        in_specs = [col(COL_SC_B), col(COL_SC_C), col(COL_SC_V), col(COL_SC_GATE)]
        args = [proj, proj, proj, proj]
        w_specs = [_const_spec((k, WIDTH))]
        w_args = [lp["sc_conv_w"]]
        body = _sc_kernel
    if not zero_state:
        in_specs.append(pl.BlockSpec((1, nb, k - 1, WIDTH), lambda b, c: (l, b, 0, 0)))
        args.append(state)
    in_specs += w_specs
    args += w_args
    s_shape, s_spec, alias_in = _stacked_out(l, nbatch, nb, (k - 1, WIDTH), prev)
    aliases = {len(args) + i: 1 + i for i in range(len(alias_in))}
    in_specs += [pl.BlockSpec(memory_space=pl.ANY)] * len(alias_in)
    scratch = [pltpu.VMEM((nb, p + tb, WIDTH), F32), pltpu.VMEM((rl, WIDTH), F32)]
    if kind == "cf" and nb == 1 and tb % 32 == 0:
        scratch.append(pltpu.VMEM((SUBLANES - 1, p + tb - SUBLANES, WIDTH), F32))
    return pl.pallas_call(
        functools.partial(body, tb=tb, nc=nc, nb=nb, p=p, zero_state=zero_state, n_alias=len(alias_in)),
        grid=(nbatch // nb, nc),
        in_specs=in_specs,
        out_specs=[pl.BlockSpec((rl, WIDTH), lambda b, c: (rows(b, c), 0)), s_spec],
        out_shape=[jax.ShapeDtypeStruct((m, WIDTH), BF16), s_shape],
        scratch_shapes=scratch,
        input_output_aliases=aliases,
        compiler_params=_cparams(("parallel", "arbitrary")),
        name=kind + "_branch",
    )(*args, *alias_in)


def _sink_column(sink_ref, g, rows_per_head):
    n = ATT_REP * rows_per_head
    rid = lax.broadcasted_iota(jnp.int32, (n, 1), 0)
    col = jnp.full((n, 1), sink_ref[g * ATT_REP + ATT_REP - 1], F32)
    for r in range(ATT_REP - 2, -1, -1):
        col = jnp.where(rid < (r + 1) * rows_per_head, sink_ref[g * ATT_REP + r], col)
    return col


def _softmax_sink(s, ok, sink_col):
    s = jnp.where(ok, s, -jnp.inf)
    m = jnp.maximum(jnp.max(s, axis=-1, keepdims=True), sink_col)
    p = jnp.exp(s - m)
    denom = jnp.sum(p, axis=-1, keepdims=True) + jnp.exp(sink_col - m)
    return p / denom


def _attn_prompt_kernel(q_ref, kp_ref, kc_ref, vp_ref, vc_ref, gate_ref, sink_ref, *rest, nblk, n_alias):
    o_ref, kw_ref, vw_ref = rest[n_alias:]
    L = WINDOW
    i = pl.program_id(1)
    q = q_ref[...].astype(F32) * (ATT_HEAD_DIM ** -0.5)
    k2 = jnp.concatenate([kp_ref[...], kc_ref[...]], axis=0).astype(F32)
    v2 = jnp.concatenate([vp_ref[...], vc_ref[...]], axis=0).astype(F32)
    li = lax.broadcasted_iota(jnp.int32, (ATT_REP * L, 2 * L), 0) & (L - 1)
    sj = lax.broadcasted_iota(jnp.int32, (ATT_REP * L, 2 * L), 1)
    ok = (sj > li) & (sj <= li + WINDOW) & ((sj >= L) | (i > 0))
    lane = lax.broadcasted_iota(jnp.int32, (2 * L, LANES), 1)
    lo_half = lane < ATT_HEAD_DIM

    o_blocks = [None] * (WIDTH // LANES)
    for g in range(ATT_KV_HEADS):
        half = g % 2
        sl = slice((g // 2) * LANES, (g // 2 + 1) * LANES)
        keep = lo_half if half == 0 else jnp.logical_not(lo_half)
        k_use = jnp.where(keep, k2[:, sl], 0.0).astype(BF16)
        v_use = jnp.where(keep, v2[:, sl], 0.0).astype(BF16)
        parts = []
        for r in range(ATT_REP):
            h = g * ATT_REP + r
            qb = q[:, (h // 2) * LANES:(h // 2 + 1) * LANES]
            parts.append(qb if h % 2 == half else pltpu.roll(qb, ATT_HEAD_DIM, 1))
        lhs = jnp.concatenate(parts, axis=0).astype(BF16)
        s = lax.dot_general(lhs, k_use, (((1,), (1,)), ((), ())), preferred_element_type=F32)
        pn = _softmax_sink(s, ok, _sink_column(sink_ref, g, L)).astype(BF16)
        o = jnp.dot(pn, v_use, preferred_element_type=F32)
        for r in range(ATT_REP):
            h = g * ATT_REP + r
            ob = o[r * L:(r + 1) * L, :]
            if h % 2 != half:
                ob = pltpu.roll(ob, ATT_HEAD_DIM, 1)
            o_blocks[h // 2] = ob if o_blocks[h // 2] is None else o_blocks[h // 2] + ob
    o = jnp.concatenate(o_blocks, axis=1)
    o_ref[...] = (o * _silu(gate_ref[...].astype(F32))).astype(o_ref.dtype)

    @pl.when(i == nblk - 1)
    def _():
        kw_ref[0, 0] = kc_ref[...].astype(F32).T.reshape(ATT_KV_HEADS, ATT_HEAD_DIM, WINDOW)
        vw_ref[0, 0] = vc_ref[...].astype(F32).T.reshape(ATT_KV_HEADS, ATT_HEAD_DIM, WINDOW)


def _win_out(l, nbatch, nb, prev):
    return _stacked_out(l, nbatch, nb, (ATT_KV_HEADS, ATT_HEAD_DIM, WINDOW), prev)


def attn_prompt(proj, sinks, l, *, nbatch, t, prev=None):
    L = WINDOW
    nblk = t // L

    def col(cidx):
        return pl.BlockSpec((L, WIDTH), lambda b, i: (b * nblk + i, cidx))

    def kv(cidx, back):
        return pl.BlockSpec((L, ATT_KV_WIDTH), lambda b, i: (b * nblk + jnp.maximum(i - back, 0), cidx))

    k_shape, k_spec, k_prev = _win_out(l, nbatch, 1, prev and prev[0])
    v_shape, v_spec, v_prev = _win_out(l, nbatch, 1, prev and prev[1])
    alias_in = k_prev + v_prev
    args = [proj, proj, proj, proj, proj, proj, sinks]
    aliases = {len(args) + i: 1 + i for i in range(len(alias_in))}
    y, kw, vw = pl.pallas_call(
        functools.partial(_attn_prompt_kernel, nblk=nblk, n_alias=len(alias_in)),
        grid=(nbatch, nblk),
        in_specs=[col(COL_Q), kv(COL_K, 1), kv(COL_K, 0), kv(COL_V, 1), kv(COL_V, 0),
                  col(COL_ATT_GATE), pl.BlockSpec(memory_space=pltpu.SMEM)]
        + [pl.BlockSpec(memory_space=pl.ANY)] * len(alias_in),
        out_specs=[pl.BlockSpec((L, WIDTH), lambda b, i: (b * nblk + i, 0)), k_spec, v_spec],
        out_shape=[jax.ShapeDtypeStruct((nbatch * t, WIDTH), BF16), k_shape, v_shape],
        input_output_aliases=aliases,
        compiler_params=_cparams(("parallel", "arbitrary")),
        name="attn_prompt",
    )(*args, *alias_in)
    return y, (kw, vw)


def _attn_sample_kernel(q_ref, kn_ref, vn_ref, ck_ref, cv_ref, gate_ref, sink_ref, *rest, L, nb, n_alias):
    o_ref, kw_ref, vw_ref = rest[n_alias:]
    q = q_ref[...].astype(F32) * (ATT_HEAD_DIM ** -0.5)
    kn_t = _pad_rows(kn_ref[...].astype(F32), WINDOW).T
    vn_t = _pad_rows(vn_ref[...].astype(F32), WINDOW).T
    nkeys = 2 * WINDOW
    li = lax.broadcasted_iota(jnp.int32, (ATT_REP * L, nkeys), 0) & (L - 1)
    sj = lax.broadcasted_iota(jnp.int32, (ATT_REP * L, nkeys), 1)
    ok_cache = (sj < WINDOW) & (sj > li)
    lane = lax.broadcasted_iota(jnp.int32, (ATT_HEAD_DIM, WINDOW), 1)
    is_new = lane >= WINDOW - L
    sink_cols = [_sink_column(sink_ref, g, L) for g in range(ATT_KV_HEADS)]
    outs = []
    for bi in range(nb):
        rows = slice(bi * L, (bi + 1) * L)
        cj = sj - (WINDOW + bi * L)
        ok = ok_cache | ((cj >= 0) & (cj <= li))
        shift_new = (WINDOW - L - bi * L) % WINDOW
        pieces = [None] * ATT_HEADS
        for g in range(ATT_KV_HEADS):
            gl = slice(g * ATT_HEAD_DIM, (g + 1) * ATT_HEAD_DIM)
            kc, vc = ck_ref[0, bi, g], cv_ref[0, bi, g]
            kn_g, vn_g = kn_t[gl, :], vn_t[gl, :]
            k2 = jnp.concatenate([kc, kn_g], axis=1).astype(BF16)
            v2 = jnp.concatenate([vc, vn_g], axis=1).astype(BF16)
            lhs = jnp.concatenate(
                [q[rows, (g * ATT_REP + r) * ATT_HEAD_DIM:(g * ATT_REP + r + 1) * ATT_HEAD_DIM]
                 for r in range(ATT_REP)], axis=0).astype(BF16)
            s = jnp.dot(lhs, k2, preferred_element_type=F32)
            pn = _softmax_sink(s, ok, sink_cols[g]).astype(BF16)
            o = lax.dot_general(pn, v2, (((1,), (1,)), ((), ())), preferred_element_type=F32)
            for r in range(ATT_REP):
                pieces[g * ATT_REP + r] = o[r * L:(r + 1) * L, :]
            for c_old, n_g, w_ref in ((kc, kn_g, kw_ref), (vc, vn_g, vw_ref)):
                tail = n_g if shift_new == 0 else pltpu.roll(n_g, shift_new, 1)
                w_ref[0, bi, g] = jnp.where(is_new, tail, pltpu.roll(c_old, WINDOW - L, 1))
        outs.append(jnp.concatenate(pieces, axis=1))
    o = jnp.concatenate(outs, axis=0)
    o_ref[...] = (o * _silu(gate_ref[...].astype(F32))).astype(o_ref.dtype)


def attn_sample(proj, sinks, cache_k, cache_v, l, *, nbatch, t, prev=None):
    nb = WINDOW // t if nbatch % (WINDOW // t) == 0 else nbatch
    rl = nb * t

    def col(cidx, w):
        return pl.BlockSpec((rl, w), lambda b: (b, cidx))

    cache = pl.BlockSpec((1, nb, ATT_KV_HEADS, ATT_HEAD_DIM, WINDOW), lambda b: (l, b, 0, 0, 0))
    k_shape, k_spec, k_prev = _win_out(l, nbatch, nb, prev and prev[0])
    v_shape, v_spec, v_prev = _win_out(l, nbatch, nb, prev and prev[1])
    alias_in = k_prev + v_prev
    args = [proj, proj, proj, cache_k, cache_v, proj, sinks]
    aliases = {len(args) + i: 1 + i for i in range(len(alias_in))}
    y, kw, vw = pl.pallas_call(
        functools.partial(_attn_sample_kernel, L=t, nb=nb, n_alias=len(alias_in)),
        grid=(nbatch // nb,),
        in_specs=[col(COL_Q, WIDTH), col(COL_K, ATT_KV_WIDTH), col(COL_V, ATT_KV_WIDTH), cache, cache,
                  col(COL_ATT_GATE, WIDTH), pl.BlockSpec(memory_space=pltpu.SMEM)]
        + [pl.BlockSpec(memory_space=pl.ANY)] * len(alias_in),
        out_specs=[pl.BlockSpec((rl, WIDTH), lambda b: (b, 0)), k_spec, v_spec],
        out_shape=[jax.ShapeDtypeStruct((nbatch * t, WIDTH), BF16), k_shape, v_shape],
        input_output_aliases=aliases,
        compiler_params=_cparams(("parallel",)),
        name="attn_sample",
    )(*args, *alias_in)
    return y, (kw, vw)


def _prep_layer(l, norm_w, w_in, ssm_conv_w, ssm_conv_b, ssm_dt_bias, ssm_a_log, ssm_d, ssm_norm_w,
                w_out_ssm, cf_conv_w, cf_conv_b, cf_ln_w, cf_ln_b, w_out_cf, sc_conv_w, w_out_sc,
                att_sinks, w_out_att, w_o):
    del w_in
    pad_h = (0, LANES - SSM_HEADS)
    return dict(
        norm_w=norm_w[l],
        ssm_conv_w=ssm_conv_w[l], ssm_conv_b=ssm_conv_b[l].reshape(1, -1),
        dt_bias=jnp.pad(ssm_dt_bias[l], pad_h).reshape(1, LANES),
        a_log=jnp.pad(ssm_a_log[l], pad_h).reshape(1, LANES),
        d_skip=jnp.repeat(ssm_d[l], SSM_HEADDIM).reshape(1, WIDTH),
        ssm_norm_w=ssm_norm_w[l].reshape(1, WIDTH),
        cf_conv_w=cf_conv_w[l], cf_conv_b=cf_conv_b[l].reshape(1, WIDTH),
        cf_ln_w=cf_ln_w[l].reshape(1, WIDTH), cf_ln_b=cf_ln_b[l].reshape(1, WIDTH),
        sc_conv_w=sc_conv_w[l], sinks=att_sinks[l],
        w_out=jnp.stack([w_out_ssm[l], w_out_cf[l], w_out_sc[l], w_out_att[l]]).astype(BF16),
        w_o=w_o[l].astype(BF16),
    )


def _layer(x, xn, w_t, lp, consts, l, states, prev, next_norm_w, *, nbatch, t):
    kw = dict(nbatch=nbatch, t=t)
    proj = in_proj(xn, w_t, l, tm=2048)
    if states is None:
        ya, st_a = ssd_branch(proj, xn, w_t, lp, consts, l, prev=prev and prev[0], **kw)
        yb, st_b = conv_branch("cf", proj, lp, l, prev=prev and prev[1], **kw)
        yc, st_c = conv_branch("sc", proj, lp, l, prev=prev and prev[2], **kw)
        yd, st_d = attn_prompt(proj, lp["sinks"], l, prev=prev and prev[3], **kw)
    else:
        ssm, conv_ssm, conv_cf, conv_sc, cache_k, cache_v = states
        ya, st_a = ssd_branch(proj, xn, w_t, lp, consts, l, conv_state=conv_ssm, ssm_state=ssm,
                              prev=prev and prev[0], **kw)
        yb, st_b = conv_branch("cf", proj, lp, l, state=conv_cf, prev=prev and prev[1], **kw)
        yc, st_c = conv_branch("sc", proj, lp, l, state=conv_sc, prev=prev and prev[2], **kw)
        yd, st_d = attn_sample(proj, lp["sinks"], cache_k, cache_v, l, prev=prev and prev[3], **kw)
    hmix = merge_branches((ya, yb, yc, yd), lp["w_out"], proj, tm=1024, tn=512)
    outs = out_proj_norm(hmix, lp["w_o"], x, next_norm_w, EPS, last=l == DEPTH - 1, tm=512)
    return outs, (st_a, st_b, st_c, st_d)


def kernel(x_prompt, x_sample, state_ssm, state_conv_ssm, state_conv_cf, state_conv_sc, cache_k, cache_v, norm_w, w_in, ssm_conv_w, ssm_conv_b, ssm_dt_bias, ssm_a_log, ssm_d, ssm_norm_w, w_out_ssm, cf_conv_w, cf_conv_b, cf_ln_w, cf_ln_b, w_out_cf, sc_conv_w, w_out_sc, att_sinks, w_out_att, w_o, final_norm_w):
    bp, tp, _ = x_prompt.shape
    bs, ts, _ = x_sample.shape
    xp = x_prompt.reshape(bp * tp, D_MODEL)
    xs = x_sample.reshape(bs * ts, D_MODEL)

    tri = jnp.tril(jnp.ones((SSM_CHUNK, SSM_CHUNK), F32))
    expand = (jnp.arange(WIDTH)[None, :] // SSM_HEADDIM == jnp.arange(LANES)[:, None]).astype(F32)
    consts = (tri, expand)
    to_win_minor = lambda c: jnp.transpose(c, (0, 1, 3, 4, 2))
    from_win_minor = lambda c: jnp.transpose(c, (0, 1, 4, 2, 3))
    w_t = jnp.swapaxes(w_in, 1, 2)
    s_states = (state_ssm.reshape(DEPTH, bs, WIDTH, SSM_STATE), state_conv_ssm, state_conv_cf,
                state_conv_sc, to_win_minor(cache_k), to_win_minor(cache_v))

    xnp = rmsnorm(xp, norm_w[0], EPS, BF16)
    xns = rmsnorm(xs, norm_w[0], EPS, BF16)
    p_prev = s_prev = None
    for l in range(DEPTH):
        lp = _prep_layer(l, norm_w, w_in, ssm_conv_w, ssm_conv_b, ssm_dt_bias, ssm_a_log, ssm_d,
                         ssm_norm_w, w_out_ssm, cf_conv_w, cf_conv_b, cf_ln_w, cf_ln_b, w_out_cf,
                         sc_conv_w, w_out_sc, att_sinks, w_out_att, w_o)
        next_w = norm_w[l + 1] if l + 1 < DEPTH else final_norm_w
        outp, p_prev = _layer(xp, xnp, w_t, lp, consts, l, None, p_prev, next_w, nbatch=bp, t=tp)
        outs, s_prev = _layer(xs, xns, w_t, lp, consts, l, s_states, s_prev, next_w, nbatch=bs, t=ts)
        if l + 1 < DEPTH:
            (xp, xnp), (xs, xns) = outp, outs
    y_prompt = outp[0].reshape(bp, tp, D_MODEL)
    y_sample = outs[0].reshape(bs, ts, D_MODEL)

    def unpack(st, nbatch):
        (h, conv_ssm), conv_cf, conv_sc, (kwin, vwin) = st
        return (h.reshape(DEPTH, nbatch, SSM_HEADS, SSM_HEADDIM, SSM_STATE), conv_ssm, conv_cf,
                conv_sc, from_win_minor(kwin), from_win_minor(vwin))

    return (y_prompt, y_sample, *unpack(p_prev, bp), *unpack(s_prev, bs))
```

```python
import functools

import jax
import jax.numpy as jnp
from jax import lax
from jax.experimental import pallas as pl
from jax.experimental.pallas import tpu as pltpu

F32 = jnp.float32
BF16 = jnp.bfloat16

D_MODEL = 2048
DEPTH = 2
N_BRANCH = 4
WIDTH = 1024
SSM_HEADS = 16
SSM_HEADDIM = 64
SSM_GROUPS = 4
SSM_STATE = 128
SSM_CONV = 4
SSM_CHUNK = 128
CF_CONV = 31
SC_CONV = 3
ATT_HEADS = 16
ATT_KV_HEADS = 4
ATT_HEAD_DIM = 64
ATT_KV_WIDTH = ATT_KV_HEADS * ATT_HEAD_DIM
ATT_REP = ATT_HEADS // ATT_KV_HEADS
WINDOW = 128
EPS = 1e-6

LANES = 128
SUBLANES = 8
VMEM_LIMIT = 56 * 1024 * 1024

COL_MERGE = 0
COL_Z = 8
COL_XS = 9
COL_BC = 10
COL_CF_A = 11
COL_CF_G = 12
COL_CF_GATE = 13
COL_SC_B = 14
COL_SC_C = 15
COL_SC_V = 16
COL_SC_GATE = 17
COL_Q = 18
COL_ATT_GATE = 19
COL_K = 80
COL_V = 81
PROJ_COLS = 20992


def _sigmoid(x):
    return 1.0 / (1.0 + jnp.exp(-x))


def _silu(x):
    return x * _sigmoid(x)


def _cparams(sem, limit=VMEM_LIMIT):
    return pltpu.CompilerParams(dimension_semantics=sem, vmem_limit_bytes=limit)


def _tile(m, target):
    t = min(m, target)
    while m % t or t % SUBLANES:
        t -= 1
    return t


def _const_spec(shape):
    return pl.BlockSpec(shape, lambda *_: (0,) * len(shape))


def _stacked_out(l, nbatch, nb, tail, prev):
    shape = (DEPTH, nbatch) + tail
    spec = pl.BlockSpec((1, nb) + tail, lambda b, *_: (l, b) + (0,) * len(tail))
    return jax.ShapeDtypeStruct(shape, F32), spec, ([] if prev is None else [prev])


def _rmsnorm_kernel(x_ref, w_ref, o_ref, *, eps):
    x = x_ref[...].astype(F32)
    ms = jnp.mean(x * x, axis=-1, keepdims=True)
    o_ref[...] = (x * lax.rsqrt(ms + eps) * w_ref[...]).astype(o_ref.dtype)


def rmsnorm(x, w, eps, out_dtype):
    m, d = x.shape
    tm = _tile(m, 512)
    return pl.pallas_call(
        functools.partial(_rmsnorm_kernel, eps=eps),
        grid=(m // tm,),
        in_specs=[pl.BlockSpec((tm, d), lambda i: (i, 0)), _const_spec((1, d))],
        out_specs=pl.BlockSpec((tm, d), lambda i: (i, 0)),
        out_shape=jax.ShapeDtypeStruct((m, d), out_dtype),
        compiler_params=_cparams(("parallel",)),
        name="rmsnorm",
    )(x, w.reshape(1, d))


_SEG_START = dict(z=0, xs=1024, bc=2048, dt=3072, cf_a=3088, cf_g=4112, cf_gate=5136, sc_b=6160,
                  sc_c=7184, sc_v=8208, sc_gate=9232, q=10256, k=11280, att_gate=11792, merge=12816)
PROJ_TN = 512


ACT_NONE, ACT_SIGMOID, ACT_SILU = 0, 1, 2
_SEG_ACT = dict(z=ACT_SILU, cf_g=ACT_SIGMOID, cf_gate=ACT_SILU, sc_gate=ACT_SILU, att_gate=ACT_SILU)


def _proj_block_tables():
    starts = [_SEG_START["merge"] + PROJ_TN * j for j in range(N_BRANCH * D_MODEL // PROJ_TN)]
    acts = [ACT_SIGMOID] * len(starts)
    for name in ("z", "xs", "bc", "cf_a", "cf_g", "cf_gate", "sc_b", "sc_c", "sc_v", "sc_gate", "q",
                 "att_gate"):
        starts += [_SEG_START[name], _SEG_START[name] + PROJ_TN]
        acts += [_SEG_ACT.get(name, ACT_NONE)] * 2
    starts.append(_SEG_START["k"])
    acts.append(ACT_NONE)
    assert len(starts) * PROJ_TN == PROJ_COLS and all(v % SUBLANES == 0 for v in starts)
    return jnp.array([v // SUBLANES for v in starts], jnp.int32), jnp.array(acts, jnp.int32)


def _inproj_kernel(starts_ref, acts_ref, a_ref, w_ref, o_ref, acc_ref):
    del starts_ref
    w = w_ref[0].astype(BF16)
    acc_ref[...] = lax.dot_general(a_ref[...], w, (((1,), (1,)), ((), ())),
                                   preferred_element_type=F32)
    act = acts_ref[pl.program_id(1)]

    @pl.when(act == ACT_NONE)
    def _():
        o_ref[...] = acc_ref[...].astype(o_ref.dtype)

    @pl.when(act == ACT_SIGMOID)
    def _():
        o_ref[...] = _sigmoid(acc_ref[...]).astype(o_ref.dtype)

    @pl.when(act == ACT_SILU)
    def _():
        o_ref[...] = _silu(acc_ref[...]).astype(o_ref.dtype)


def in_proj(xn, w_t, l, tm):
    m, k = xn.shape
    tm = _tile(m, tm)
    grid_spec = pltpu.PrefetchScalarGridSpec(
        num_scalar_prefetch=2, grid=(m // tm, PROJ_COLS // PROJ_TN),
        in_specs=[pl.BlockSpec((tm, k), lambda i, j, s, a: (i, 0)),
                  pl.BlockSpec((pl.Element(1), pl.Element(PROJ_TN), pl.Element(k)),
                               lambda i, j, s, a: (l, s[j] * SUBLANES, 0))],
        out_specs=pl.BlockSpec((tm, PROJ_TN), lambda i, j, s, a: (i, j)),
        scratch_shapes=[pltpu.VMEM((tm, PROJ_TN), F32)])
    return pl.pallas_call(
        _inproj_kernel, grid_spec=grid_spec,
        out_shape=jax.ShapeDtypeStruct((m, PROJ_COLS), BF16),
        compiler_params=_cparams(("parallel", "arbitrary")),
        name="in_proj",
    )(*_proj_block_tables(), xn, w_t)


def _out_kernel(h_ref, w_ref, x_ref, nw_ref, *out_refs, eps, last):
    x = x_ref[...] + jnp.dot(h_ref[...], w_ref[...], preferred_element_type=F32)
    ms = jnp.mean(x * x, axis=-1, keepdims=True)
    xn = x * lax.rsqrt(ms + eps) * nw_ref[...]
    if last:
        out_refs[0][...] = xn
    else:
        out_refs[0][...] = x
        out_refs[1][...] = xn.astype(out_refs[1].dtype)


def out_proj_norm(h, w, x, norm_w, eps, last, tm):
    m, k = h.shape
    n = w.shape[1]
    tm = _tile(m, tm)
    row = lambda i: (i, 0)
    if last:
        out_shape = [jax.ShapeDtypeStruct((m, n), F32)]
    else:
        out_shape = [jax.ShapeDtypeStruct((m, n), F32), jax.ShapeDtypeStruct((m, n), BF16)]
    return pl.pallas_call(
        functools.partial(_out_kernel, eps=eps, last=last),
        grid=(m // tm,),
        in_specs=[pl.BlockSpec((tm, k), row),
                  pl.BlockSpec((k, n), lambda i: (0, 0), pipeline_mode=pl.Buffered(1)),
                  pl.BlockSpec((tm, n), row), _const_spec((1, n))],
        out_specs=[pl.BlockSpec((tm, n), row)] * len(out_shape),
        out_shape=out_shape,
        compiler_params=_cparams(("parallel",)),
        name="out_proj_norm",
    )(h, w, x, norm_w.reshape(1, n))


def _merge_kernel(ba_ref, bb_ref, bc_ref, bd_ref, w_ref, m0_ref, m1_ref, m2_ref, m3_ref, o_ref):
    acc = None
    for i, (b_ref, m_ref) in enumerate(((ba_ref, m0_ref), (bb_ref, m1_ref),
                                        (bc_ref, m2_ref), (bd_ref, m3_ref))):
        out = jnp.dot(b_ref[...], w_ref[i], preferred_element_type=F32)
        term = m_ref[...].astype(F32) * out
        acc = term if acc is None else acc + term
    o_ref[...] = acc.astype(o_ref.dtype)


def merge_branches(branches, w_out, proj, tm, tn):
    m = proj.shape[0]
    tm = _tile(m, tm)
    nb = D_MODEL // tn
    br_spec = pl.BlockSpec((tm, WIDTH), lambda i, j: (i, 0))
    merge_specs = [pl.BlockSpec((tm, tn), functools.partial(lambda i, j, b: (i, b * nb + j), b=b))
                   for b in range(N_BRANCH)]
    return pl.pallas_call(
        _merge_kernel,
        grid=(m // tm, nb),
        in_specs=[br_spec] * 4 + [pl.BlockSpec((N_BRANCH, WIDTH, tn), lambda i, j: (0, 0, j))] + merge_specs,
        out_specs=pl.BlockSpec((tm, tn), lambda i, j: (i, j)),
        out_shape=jax.ShapeDtypeStruct((m, D_MODEL), BF16),
        compiler_params=_cparams(("parallel", "arbitrary")),
        name="merge_branches",
    )(*branches, w_out, proj, proj, proj, proj)


def _pad_rows(x, rows):
    if x.shape[0] == rows:
        return x
    return jnp.concatenate([x, jnp.zeros((rows - x.shape[0], x.shape[1]), x.dtype)], axis=0)


def _bf16_terms(x, n):
    terms = []
    for _ in range(n):
        t = x.astype(BF16)
        terms.append(t)
        x = x - t.astype(F32)
    return terms


def _head_stats(dtr_all, dtb, alog, tri3, e2, *, L, nb):
    dtr_all = dtr_all + dtb
    dt_all = jnp.maximum(dtr_all, 0.0) + jnp.log1p(jnp.exp(-jnp.abs(dtr_all)))
    da_all = dt_all * (-jnp.exp(alog))
    css, ecss, decs = [], [], []
    for bi in range(nb):
        da = da_all[bi * L:(bi + 1) * L, :]
        if L == SSM_CHUNK:
            cs = jnp.dot(tri3, jnp.concatenate(_bf16_terms(da, 3), axis=0), preferred_element_type=F32)
        else:
            row = lax.broadcasted_iota(jnp.int32, (L, LANES), 0)
            cs = jnp.zeros((L, LANES), F32)
            for s in range(L):
                cs = cs + jnp.where(row >= s, da[s:s + 1, :], 0.0)
        css.append(cs)
        ecss.append(jnp.exp(cs))
        decs.append(jnp.exp(cs[L - 1:L, :] - cs))
    stack = jnp.concatenate([dt_all] + ecss + decs, axis=0)
    ex = jnp.dot(jnp.concatenate(_bf16_terms(stack, 2), axis=1), e2, preferred_element_type=F32)
    n = nb * L
    return [(css[bi], ex[bi * L:(bi + 1) * L], ex[n + bi * L:n + (bi + 1) * L],
             ex[2 * n + bi * L:2 * n + (bi + 1) * L]) for bi in range(nb)]


def _ssd_core(xbc, z_act, stats, h_ref, dsk, nw, *, L):
    cs, dt_e, ecs_e, dec_e = stats
    mm = BF16 if L >= 16 else F32
    xs = xbc[:, 0:WIDTH]
    bmat = xbc[:, WIDTH:WIDTH + SSM_GROUPS * SSM_STATE]
    cmat = xbc[:, WIDTH + SSM_GROUPS * SSM_STATE:]
    xdt = xs * dt_e
    xdd = xdt * dec_e

    lp = SSM_CHUNK
    xdt_p = _pad_rows(xdt, lp).astype(mm)
    xdd_p = _pad_rows(xdd, lp)
    b_p = _pad_rows(bmat, lp).astype(mm)
    cs_t = _pad_rows(cs, lp).T
    cmat_m = cmat.astype(mm)

    li = lax.broadcasted_iota(jnp.int32, (L, lp), 0)
    sj = lax.broadcasted_iota(jnp.int32, (L, lp), 1)
    causal = sj <= li
    lane = lax.broadcasted_iota(jnp.int32, (lp, LANES), 1)
    lo_half = lane < SSM_HEADDIM

    y_blocks = []
    hpg = SSM_HEADS // SSM_GROUPS
    gw = hpg * SSM_HEADDIM
    bpg = gw // LANES
    for g in range(SSM_GROUPS):
        c_g = cmat_m[:, g * SSM_STATE:(g + 1) * SSM_STATE]
        b_g = b_p[:, g * SSM_STATE:(g + 1) * SSM_STATE]
        cb = lax.dot_general(c_g, b_g, (((1,), (1,)), ((), ())), preferred_element_type=F32)
        h_blk = h_ref[g * gw:(g + 1) * gw, :]
        y_off = lax.dot_general(c_g, h_blk.astype(mm), (((1,), (1,)), ((), ())),
                                preferred_element_type=F32)
        y_off = y_off * ecs_e[:, g * gw:(g + 1) * gw]
        for jb in range(bpg):
            blk = g * bpg + jb
            x_blk = xdt_p[:, blk * LANES:(blk + 1) * LANES]
            acc = None
            for par in range(2):
                h = 2 * blk + par
                diff = cs[:, h:h + 1] - cs_t[h:h + 1, :]
                lm = jnp.exp(jnp.where(causal, diff, -jnp.inf))
                gmat = (cb * lm).astype(mm)
                x_half = jnp.where(lo_half if par == 0 else jnp.logical_not(lo_half), x_blk,
                                   jnp.zeros_like(x_blk))
                t = jnp.dot(gmat, x_half, preferred_element_type=F32)
                acc = t if acc is None else acc + t
            y_blocks.append(acc + y_off[:, jb * LANES:(jb + 1) * LANES])
        xdd_t = jnp.concatenate(
            [xdd_p[:, (g * bpg + jb) * LANES:(g * bpg + jb + 1) * LANES].T for jb in range(bpg)],
            axis=0).astype(mm)
        s_new = jnp.dot(xdd_t, b_g, preferred_element_type=F32)
        for r in range(hpg):
            h = g * hpg + r
            scal = jnp.exp(cs_t[h:h + 1, L - 1:L])
            rows = slice(h * SSM_HEADDIM, (h + 1) * SSM_HEADDIM)
            h_ref[rows, :] = h_ref[rows, :] * scal + s_new[r * SSM_HEADDIM:(r + 1) * SSM_HEADDIM, :]

    y = jnp.concatenate(y_blocks, axis=1) + dsk * xs
    y = y * z_act
    ms = jnp.mean(y * y, axis=-1, keepdims=True)
    return y * lax.rsqrt(ms + 1e-5) * nw


def _ssd_kernel(*refs, L, nc, nb, zero_state, n_alias):
    xs_ref, bc_ref, z_ref, xn_ref, wdt_ref = refs[:5]
    refs = refs[5:]
    if not zero_state:
        pre_ref, h0_ref = refs[:2]
        refs = refs[2:]
    cw_ref, cb_ref, dtb_ref, alog_ref, dsk_ref, nw_ref, tri_ref, e_ref = refs[:8]
    y_ref, hout_ref, cst_ref, xp_ref, h_ref = refs[8 + n_alias:]
    c = pl.program_id(1)
    off = SUBLANES - (SSM_CONV - 1)

    dtr_all = lax.dot_general(xn_ref[...], wdt_ref[0].astype(BF16), (((1,), (1,)), ((), ())),
                              preferred_element_type=F32)
    stats = _head_stats(dtr_all, dtb_ref[...], alog_ref[...], tri_ref[...], e_ref[...], L=L, nb=nb)
    xs_all = xs_ref[...].astype(F32)
    bc_all = bc_ref[...].astype(F32)
    z_all = z_ref[...].astype(F32)
    ys = []
    for bi in range(nb):
        rows = slice(bi * L, (bi + 1) * L)

        @pl.when(c == 0)
        def _():
            if zero_state:
                xp_ref[bi, off:SUBLANES, :] = jnp.zeros((SSM_CONV - 1, 2 * WIDTH), F32)
                h_ref[bi] = jnp.zeros((WIDTH, SSM_STATE), F32)
            else:
                xp_ref[bi, off:SUBLANES, :] = pre_ref[0, bi]
                h_ref[bi] = h0_ref[0, bi]

        if nc > 1:
            @pl.when(c > 0)
            def _():
                xp_ref[bi, off:SUBLANES, :] = xp_ref[bi, L + off:L + SUBLANES, :]

        xp_ref[bi, SUBLANES:SUBLANES + L, 0:WIDTH] = xs_all[rows, :]
        xp_ref[bi, SUBLANES:SUBLANES + L, WIDTH:2 * WIDTH] = bc_all[rows, :]

        conv = cb_ref[...] + cw_ref[0:1, :] * xp_ref[bi, off:off + L, :]
        for k in range(1, SSM_CONV):
            conv = conv + cw_ref[k:k + 1, :] * xp_ref[bi, off + k:off + k + L, :]
        ys.append(_ssd_core(_silu(conv), z_all[rows, :], stats[bi], h_ref.at[bi], dsk_ref[...],
                            nw_ref[...], L=L))

        @pl.when(c == nc - 1)
        def _():
            hout_ref[0, bi] = h_ref[bi]
            cst_ref[0, bi] = xp_ref[bi, L + off:L + SUBLANES, :]

    y_ref[...] = jnp.concatenate(ys, axis=0).astype(y_ref.dtype)


def ssd_branch(proj, xn, w_t, lp, consts, l, *, nbatch, t, conv_state=None, ssm_state=None, prev=None):
    L = SSM_CHUNK if t % SSM_CHUNK == 0 else t
    nc = t // L
    zero_state = conv_state is None
    nb = 4 if nc == 1 and nbatch % 4 == 0 else 1
    m = nbatch * t
    rl = nb * L

    def rows(b, c):
        return b * nc + c

    def col(cidx):
        return pl.BlockSpec((rl, WIDTH), lambda b, c: (rows(b, c), cidx))

    tri, e = consts
    in_specs = [col(COL_XS), col(COL_BC), col(COL_Z),
                pl.BlockSpec((rl, D_MODEL), lambda b, c: (rows(b, c), 0)),
                pl.BlockSpec((1, LANES, D_MODEL), lambda b, c: (l, _SEG_START["dt"] // LANES, 0))]
    args = [proj, proj, proj, xn, w_t]
    if not zero_state:
        in_specs += [pl.BlockSpec((1, nb, SSM_CONV - 1, 2 * WIDTH), lambda b, c: (l, b, 0, 0)),
                     pl.BlockSpec((1, nb, WIDTH, SSM_STATE), lambda b, c: (l, b, 0, 0))]
        args += [conv_state, ssm_state]
    in_specs += [_const_spec((SSM_CONV, 2 * WIDTH)), _const_spec((1, 2 * WIDTH)),
                 _const_spec((1, LANES)), _const_spec((1, LANES)), _const_spec((1, WIDTH)),
                 _const_spec((1, WIDTH)), _const_spec((SSM_CHUNK, 3 * SSM_CHUNK)),
                 _const_spec((2 * LANES, WIDTH))]
    args += [lp["ssm_conv_w"], lp["ssm_conv_b"], lp["dt_bias"], lp["a_log"], lp["d_skip"],
             lp["ssm_norm_w"], tri, e]
    h_shape, h_spec, h_prev = _stacked_out(l, nbatch, nb, (WIDTH, SSM_STATE), prev and prev[0])
    c_shape, c_spec, c_prev = _stacked_out(l, nbatch, nb, (SSM_CONV - 1, 2 * WIDTH), prev and prev[1])
    alias_in = h_prev + c_prev
    aliases = {len(args) + i: 1 + i for i in range(len(alias_in))}
    in_specs += [pl.BlockSpec(memory_space=pl.ANY)] * len(alias_in)
    y, h_out, c_out = pl.pallas_call(
        functools.partial(_ssd_kernel, L=L, nc=nc, nb=nb, zero_state=zero_state, n_alias=len(alias_in)),
        grid=(nbatch // nb, nc),
        in_specs=in_specs,
        out_specs=[pl.BlockSpec((rl, WIDTH), lambda b, c: (rows(b, c), 0)), h_spec, c_spec],
        out_shape=[jax.ShapeDtypeStruct((m, WIDTH), BF16), h_shape, c_shape],
        scratch_shapes=[pltpu.VMEM((nb, SUBLANES + L, 2 * WIDTH), F32),
                        pltpu.VMEM((nb, WIDTH, SSM_STATE), F32)],
        input_output_aliases=aliases,
        compiler_params=_cparams(("parallel", "arbitrary")),
        name="ssd_branch",
    )(*args, *alias_in)
    return y, (h_out, c_out)


def _dwconv(xp_ref, xsh_ref, w_ref, b_ref, acc_ref, *, k, start, tb, row0):
    rbk = min(tb, 32)
    if xsh_ref is not None:
        nrows = xsh_ref.shape[1]
        for s in range(1, SUBLANES):
            xsh_ref[s - 1] = xp_ref[s:s + nrows, :]
    for j in range(WIDTH // LANES):
        lanes = slice(j * LANES, (j + 1) * LANES)
        for r in range(tb // rbk):
            r0 = r * rbk
            acc = None
            for i in range(k):
                first = start + i + r0
                s = first % SUBLANES
                if xsh_ref is None or s == 0:
                    x = xp_ref[first:first + rbk, lanes]
                else:
                    x = xsh_ref[s - 1, first - s:first - s + rbk, lanes]
                term = w_ref[i:i + 1, lanes] * x
                if acc is None:
                    acc = term if b_ref is None else term + b_ref[:, lanes]
                else:
                    acc = acc + term
            acc_ref[row0 + r0:row0 + r0 + rbk, lanes] = acc


def _conv_all(u, xp_ref, pre_ref, st_ref, w_ref, b_ref, acc_ref, xsh_ref=None, *, k, p, tb, nc, nb):
    c = pl.program_id(1)
    for bi in range(nb):
        @pl.when(c == 0)
        def _():
            if pre_ref is None:
                xp_ref[bi, p - (k - 1):p, :] = jnp.zeros((k - 1, WIDTH), F32)
            else:
                xp_ref[bi, p - (k - 1):p, :] = pre_ref[0, bi]

        if nc > 1:
            @pl.when(c > 0)
            def _():
                xp_ref[bi, 0:p, :] = xp_ref[bi, tb:tb + p, :]

        xp_ref[bi, p:p + tb, :] = u[bi * tb:(bi + 1) * tb, :]
        _dwconv(xp_ref.at[bi], xsh_ref, w_ref, b_ref, acc_ref, k=k, start=p - (k - 1), tb=tb,
                row0=bi * tb)

        @pl.when(c == nc - 1)
        def _():
            st_ref[0, bi] = xp_ref[bi, p + tb - (k - 1):p + tb, :]


def _cf_kernel(*refs, tb, nc, nb, p, zero_state, n_alias):
    a_ref, g_ref, gate_ref = refs[:3]
    refs = refs[3:]
    pre_ref = None
    if not zero_state:
        pre_ref, refs = refs[0], refs[1:]
    w_ref, b_ref, lnw_ref, lnb_ref = refs[:4]
    o_ref, st_ref, xp_ref, acc_ref, *xsh = refs[4 + n_alias:]
    u = a_ref[...].astype(F32) * g_ref[...].astype(F32)
    _conv_all(u, xp_ref, pre_ref, st_ref, w_ref, b_ref, acc_ref, *xsh, k=CF_CONV, p=p, tb=tb, nc=nc,
              nb=nb)
    u = acc_ref[...]
    mu = jnp.mean(u, axis=-1, keepdims=True)
    uc = u - mu
    var = jnp.mean(uc * uc, axis=-1, keepdims=True)
    u = _silu(uc * lax.rsqrt(var + 1e-5) * lnw_ref[...] + lnb_ref[...])
    o_ref[...] = (u * gate_ref[...].astype(F32)).astype(o_ref.dtype)


def _sc_kernel(*refs, tb, nc, nb, p, zero_state, n_alias):
    gb_ref, gc_ref, sv_ref, gate_ref = refs[:4]
    refs = refs[4:]
    pre_ref = None
    if not zero_state:
        pre_ref, refs = refs[0], refs[1:]
    w_ref = refs[0]
    o_ref, st_ref, xp_ref, acc_ref = refs[1 + n_alias:]
    u = gc_ref[...].astype(F32) * sv_ref[...].astype(F32)
    _conv_all(u, xp_ref, pre_ref, st_ref, w_ref, None, acc_ref, k=SC_CONV, p=p, tb=tb, nc=nc, nb=nb)
    o_ref[...] = (gb_ref[...].astype(F32) * acc_ref[...]
                  * gate_ref[...].astype(F32)).astype(o_ref.dtype)


def conv_branch(kind, proj, lp, l, *, nbatch, t, state=None, prev=None):
    tb = min(t, 128)
    nc = t // tb
    nb = 8 if nc == 1 and nbatch % 8 == 0 else 1
    m = nbatch * t
    rl = nb * tb
    zero_state = state is None
    k = CF_CONV if kind == "cf" else SC_CONV
    p = -(-(k - 1) // SUBLANES) * SUBLANES

    def rows(b, c):
        return b * nc + c

    def col(cidx):
        return pl.BlockSpec((rl, WIDTH), lambda b, c: (rows(b, c), cidx))

    if kind == "cf":
        in_specs = [col(COL_CF_A), col(COL_CF_G), col(COL_CF_GATE)]
        args = [proj, proj, proj]
        w_specs = [_const_spec((k, WIDTH)), _const_spec((1, WIDTH)), _const_spec((1, WIDTH)),
                   _const_spec((1, WIDTH))]
        w_args = [lp["cf_conv_w"], lp["cf_conv_b"], lp["cf_ln_w"], lp["cf_ln_b"]]
        body = _cf_kernel
    else:
        in_specs = [col(COL_SC_B), col(COL_SC_C), col(COL_SC_V), col(COL_SC_GATE)]
        args = [proj, proj, proj, proj]
        w_specs = [_const_spec((k, WIDTH))]
        w_args = [lp["sc_conv_w"]]
        body = _sc_kernel
    if not zero_state:
        in_specs.append(pl.BlockSpec((1, nb, k - 1, WIDTH), lambda b, c: (l, b, 0, 0)))
        args.append(state)
    in_specs += w_specs
    args += w_args
    s_shape, s_spec, alias_in = _stacked_out(l, nbatch, nb, (k - 1, WIDTH), prev)
    aliases = {len(args) + i: 1 + i for i in range(len(alias_in))}
    in_specs += [pl.BlockSpec(memory_space=pl.ANY)] * len(alias_in)
    scratch = [pltpu.VMEM((nb, p + tb, WIDTH), F32), pltpu.VMEM((rl, WIDTH), F32)]
    if kind == "cf" and nb == 1 and tb % 32 == 0:
        scratch.append(pltpu.VMEM((SUBLANES - 1, p + tb - SUBLANES, WIDTH), F32))
    return pl.pallas_call(
        functools.partial(body, tb=tb, nc=nc, nb=nb, p=p, zero_state=zero_state, n_alias=len(alias_in)),
        grid=(nbatch // nb, nc),
        in_specs=in_specs,
        out_specs=[pl.BlockSpec((rl, WIDTH), lambda b, c: (rows(b, c), 0)), s_spec],
        out_shape=[jax.ShapeDtypeStruct((m, WIDTH), BF16), s_shape],
        scratch_shapes=scratch,
        input_output_aliases=aliases,
        compiler_params=_cparams(("parallel", "arbitrary")),
        name=kind + "_branch",
    )(*args, *alias_in)


def _sink_column(sink_ref, g, rows_per_head):
    n = ATT_REP * rows_per_head
    rid = lax.broadcasted_iota(jnp.int32, (n, 1), 0)
    col = jnp.full((n, 1), sink_ref[g * ATT_REP + ATT_REP - 1], F32)
    for r in range(ATT_REP - 2, -1, -1):
        col = jnp.where(rid < (r + 1) * rows_per_head, sink_ref[g * ATT_REP + r], col)
    return col


def _softmax_sink(s, ok, sink_col):
    s = jnp.where(ok, s, -jnp.inf)
    m = jnp.maximum(jnp.max(s, axis=-1, keepdims=True), sink_col)
    p = jnp.exp(s - m)
    denom = jnp.sum(p, axis=-1, keepdims=True) + jnp.exp(sink_col - m)
    return p / denom


def _attn_prompt_kernel(q_ref, kp_ref, kc_ref, vp_ref, vc_ref, gate_ref, sink_ref, *rest, nblk, n_alias):
    o_ref, kw_ref, vw_ref = rest[n_alias:]
    L = WINDOW
    i = pl.program_id(1)
    q = q_ref[...].astype(F32) * (ATT_HEAD_DIM ** -0.5)
    k2 = jnp.concatenate([kp_ref[...], kc_ref[...]], axis=0).astype(F32)
    v2 = jnp.concatenate([vp_ref[...], vc_ref[...]], axis=0).astype(F32)
    li = lax.broadcasted_iota(jnp.int32, (ATT_REP * L, 2 * L), 0) & (L - 1)
    sj = lax.broadcasted_iota(jnp.int32, (ATT_REP * L, 2 * L), 1)
    ok = (sj > li) & (sj <= li + WINDOW) & ((sj >= L) | (i > 0))
    lane = lax.broadcasted_iota(jnp.int32, (2 * L, LANES), 1)
    lo_half = lane < ATT_HEAD_DIM

    o_blocks = [None] * (WIDTH // LANES)
    for g in range(ATT_KV_HEADS):
        half = g % 2
        sl = slice((g // 2) * LANES, (g // 2 + 1) * LANES)
        keep = lo_half if half == 0 else jnp.logical_not(lo_half)
        k_use = jnp.where(keep, k2[:, sl], 0.0).astype(BF16)
        v_use = jnp.where(keep, v2[:, sl], 0.0).astype(BF16)
        parts = []
        for r in range(ATT_REP):
            h = g * ATT_REP + r
            qb = q[:, (h // 2) * LANES:(h // 2 + 1) * LANES]
            parts.append(qb if h % 2 == half else pltpu.roll(qb, ATT_HEAD_DIM, 1))
        lhs = jnp.concatenate(parts, axis=0).astype(BF16)
        s = lax.dot_general(lhs, k_use, (((1,), (1,)), ((), ())), preferred_element_type=F32)
        pn = _softmax_sink(s, ok, _sink_column(sink_ref, g, L)).astype(BF16)
        o = jnp.dot(pn, v_use, preferred_element_type=F32)
        for r in range(ATT_REP):
            h = g * ATT_REP + r
            ob = o[r * L:(r + 1) * L, :]
            if h % 2 != half:
                ob = pltpu.roll(ob, ATT_HEAD_DIM, 1)
            o_blocks[h // 2] = ob if o_blocks[h // 2] is None else o_blocks[h // 2] + ob
    o = jnp.concatenate(o_blocks, axis=1)
    o_ref[...] = (o * gate_ref[...].astype(F32)).astype(o_ref.dtype)

    @pl.when(i == nblk - 1)
    def _():
        kw_ref[0, 0] = kc_ref[...].astype(F32).T.reshape(ATT_KV_HEADS, ATT_HEAD_DIM, WINDOW)
        vw_ref[0, 0] = vc_ref[...].astype(F32).T.reshape(ATT_KV_HEADS, ATT_HEAD_DIM, WINDOW)


def _win_out(l, nbatch, nb, prev):
    return _stacked_out(l, nbatch, nb, (ATT_KV_HEADS, ATT_HEAD_DIM, WINDOW), prev)


def attn_prompt(proj, sinks, l, *, nbatch, t, prev=None):
    L = WINDOW
    nblk = t // L

    def col(cidx):
        return pl.BlockSpec((L, WIDTH), lambda b, i: (b * nblk + i, cidx))

    def kv(cidx, back):
        return pl.BlockSpec((L, ATT_KV_WIDTH), lambda b, i: (b * nblk + jnp.maximum(i - back, 0), cidx))

    k_shape, k_spec, k_prev = _win_out(l, nbatch, 1, prev and prev[0])
    v_shape, v_spec, v_prev = _win_out(l, nbatch, 1, prev and prev[1])
    alias_in = k_prev + v_prev
    args = [proj, proj, proj, proj, proj, proj, sinks]
    aliases = {len(args) + i: 1 + i for i in range(len(alias_in))}
    y, kw, vw = pl.pallas_call(
        functools.partial(_attn_prompt_kernel, nblk=nblk, n_alias=len(alias_in)),
        grid=(nbatch, nblk),
        in_specs=[col(COL_Q), kv(COL_K, 1), kv(COL_K, 0), kv(COL_V, 1), kv(COL_V, 0),
                  col(COL_ATT_GATE), pl.BlockSpec(memory_space=pltpu.SMEM)]
        + [pl.BlockSpec(memory_space=pl.ANY)] * len(alias_in),
        out_specs=[pl.BlockSpec((L, WIDTH), lambda b, i: (b * nblk + i, 0)), k_spec, v_spec],
        out_shape=[jax.ShapeDtypeStruct((nbatch * t, WIDTH), BF16), k_shape, v_shape],
        input_output_aliases=aliases,
        compiler_params=_cparams(("parallel", "arbitrary")),
        name="attn_prompt",
    )(*args, *alias_in)
    return y, (kw, vw)


def _attn_sample_kernel(q_ref, kn_ref, vn_ref, ck_ref, cv_ref, gate_ref, sink_ref, *rest, L, nb, n_alias):
    o_ref, kw_ref, vw_ref = rest[n_alias:]
    q = q_ref[...].astype(F32) * (ATT_HEAD_DIM ** -0.5)
    kn_t = _pad_rows(kn_ref[...].astype(F32), WINDOW).T
    vn_t = _pad_rows(vn_ref[...].astype(F32), WINDOW).T
    nkeys = 2 * WINDOW
    li = lax.broadcasted_iota(jnp.int32, (ATT_REP * L, nkeys), 0) & (L - 1)
    sj = lax.broadcasted_iota(jnp.int32, (ATT_REP * L, nkeys), 1)
    ok_cache = (sj < WINDOW) & (sj > li)
    lane = lax.broadcasted_iota(jnp.int32, (ATT_HEAD_DIM, WINDOW), 1)
    is_new = lane >= WINDOW - L
    sink_cols = [_sink_column(sink_ref, g, L) for g in range(ATT_KV_HEADS)]
    outs = []
    for bi in range(nb):
        rows = slice(bi * L, (bi + 1) * L)
        cj = sj - (WINDOW + bi * L)
        ok = ok_cache | ((cj >= 0) & (cj <= li))
        shift_new = (WINDOW - L - bi * L) % WINDOW
        pieces = [None] * ATT_HEADS
        for g in range(ATT_KV_HEADS):
            gl = slice(g * ATT_HEAD_DIM, (g + 1) * ATT_HEAD_DIM)
            kc, vc = ck_ref[0, bi, g], cv_ref[0, bi, g]
            kn_g, vn_g = kn_t[gl, :], vn_t[gl, :]
            k2 = jnp.concatenate([kc, kn_g], axis=1).astype(BF16)
            v2 = jnp.concatenate([vc, vn_g], axis=1).astype(BF16)
            lhs = jnp.concatenate(
                [q[rows, (g * ATT_REP + r) * ATT_HEAD_DIM:(g * ATT_REP + r + 1) * ATT_HEAD_DIM]
                 for r in range(ATT_REP)], axis=0).astype(BF16)
            s = jnp.dot(lhs, k2, preferred_element_type=F32)
            pn = _softmax_sink(s, ok, sink_cols[g]).astype(BF16)
            o = lax.dot_general(pn, v2, (((1,), (1,)), ((), ())), preferred_element_type=F32)
            for r in range(ATT_REP):
                pieces[g * ATT_REP + r] = o[r * L:(r + 1) * L, :]
            for c_old, n_g, w_ref in ((kc, kn_g, kw_ref), (vc, vn_g, vw_ref)):
                tail = n_g if shift_new == 0 else pltpu.roll(n_g, shift_new, 1)
                w_ref[0, bi, g] = jnp.where(is_new, tail, pltpu.roll(c_old, WINDOW - L, 1))
        outs.append(jnp.concatenate(pieces, axis=1))
    o = jnp.concatenate(outs, axis=0)
    o_ref[...] = (o * gate_ref[...].astype(F32)).astype(o_ref.dtype)


def attn_sample(proj, sinks, cache_k, cache_v, l, *, nbatch, t, prev=None):
    nb = WINDOW // t if nbatch % (WINDOW // t) == 0 else nbatch
    rl = nb * t

    def col(cidx, w):
        return pl.BlockSpec((rl, w), lambda b: (b, cidx))

    cache = pl.BlockSpec((1, nb, ATT_KV_HEADS, ATT_HEAD_DIM, WINDOW), lambda b: (l, b, 0, 0, 0))
    k_shape, k_spec, k_prev = _win_out(l, nbatch, nb, prev and prev[0])
    v_shape, v_spec, v_prev = _win_out(l, nbatch, nb, prev and prev[1])
    alias_in = k_prev + v_prev
    args = [proj, proj, proj, cache_k, cache_v, proj, sinks]
    aliases = {len(args) + i: 1 + i for i in range(len(alias_in))}
    y, kw, vw = pl.pallas_call(
        functools.partial(_attn_sample_kernel, L=t, nb=nb, n_alias=len(alias_in)),
        grid=(nbatch // nb,),
        in_specs=[col(COL_Q, WIDTH), col(COL_K, ATT_KV_WIDTH), col(COL_V, ATT_KV_WIDTH), cache, cache,
                  col(COL_ATT_GATE, WIDTH), pl.BlockSpec(memory_space=pltpu.SMEM)]
        + [pl.BlockSpec(memory_space=pl.ANY)] * len(alias_in),
        out_specs=[pl.BlockSpec((rl, WIDTH), lambda b: (b, 0)), k_spec, v_spec],
        out_shape=[jax.ShapeDtypeStruct((nbatch * t, WIDTH), BF16), k_shape, v_shape],
        input_output_aliases=aliases,
        compiler_params=_cparams(("parallel",)),
        name="attn_sample",
    )(*args, *alias_in)
    return y, (kw, vw)


def _prep_layer(l, norm_w, w_in, ssm_conv_w, ssm_conv_b, ssm_dt_bias, ssm_a_log, ssm_d, ssm_norm_w,
                w_out_ssm, cf_conv_w, cf_conv_b, cf_ln_w, cf_ln_b, w_out_cf, sc_conv_w, w_out_sc,
                att_sinks, w_out_att, w_o):
    del w_in
    pad_h = (0, LANES - SSM_HEADS)
    return dict(
        norm_w=norm_w[l],
        ssm_conv_w=ssm_conv_w[l], ssm_conv_b=ssm_conv_b[l].reshape(1, -1),
        dt_bias=jnp.pad(ssm_dt_bias[l], pad_h).reshape(1, LANES),
        a_log=jnp.pad(ssm_a_log[l], pad_h).reshape(1, LANES),
        d_skip=jnp.repeat(ssm_d[l], SSM_HEADDIM).reshape(1, WIDTH),
        ssm_norm_w=ssm_norm_w[l].reshape(1, WIDTH),
        cf_conv_w=cf_conv_w[l], cf_conv_b=cf_conv_b[l].reshape(1, WIDTH),
        cf_ln_w=cf_ln_w[l].reshape(1, WIDTH), cf_ln_b=cf_ln_b[l].reshape(1, WIDTH),
        sc_conv_w=sc_conv_w[l], sinks=att_sinks[l],
        w_out=jnp.stack([w_out_ssm[l], w_out_cf[l], w_out_sc[l], w_out_att[l]]).astype(BF16),
        w_o=w_o[l].astype(BF16),
    )


def _layer(x, xn, w_t, lp, consts, l, states, prev, next_norm_w, *, nbatch, t):
    kw = dict(nbatch=nbatch, t=t)
    proj = in_proj(xn, w_t, l, tm=2048)
    if states is None:
        ya, st_a = ssd_branch(proj, xn, w_t, lp, consts, l, prev=prev and prev[0], **kw)
        yb, st_b = conv_branch("cf", proj, lp, l, prev=prev and prev[1], **kw)
        yc, st_c = conv_branch("sc", proj, lp, l, prev=prev and prev[2], **kw)
        yd, st_d = attn_prompt(proj, lp["sinks"], l, prev=prev and prev[3], **kw)
    else:
        ssm, conv_ssm, conv_cf, conv_sc, cache_k, cache_v = states
        ya, st_a = ssd_branch(proj, xn, w_t, lp, consts, l, conv_state=conv_ssm, ssm_state=ssm,
                              prev=prev and prev[0], **kw)
        yb, st_b = conv_branch("cf", proj, lp, l, state=conv_cf, prev=prev and prev[1], **kw)
        yc, st_c = conv_branch("sc", proj, lp, l, state=conv_sc, prev=prev and prev[2], **kw)
        yd, st_d = attn_sample(proj, lp["sinks"], cache_k, cache_v, l, prev=prev and prev[3], **kw)
    hmix = merge_branches((ya, yb, yc, yd), lp["w_out"], proj, tm=1024, tn=512)
    outs = out_proj_norm(hmix, lp["w_o"], x, next_norm_w, EPS, last=l == DEPTH - 1, tm=512)
    return outs, (st_a, st_b, st_c, st_d)


def kernel(x_prompt, x_sample, state_ssm, state_conv_ssm, state_conv_cf, state_conv_sc, cache_k, cache_v, norm_w, w_in, ssm_conv_w, ssm_conv_b, ssm_dt_bias, ssm_a_log, ssm_d, ssm_norm_w, w_out_ssm, cf_conv_w, cf_conv_b, cf_ln_w, cf_ln_b, w_out_cf, sc_conv_w, w_out_sc, att_sinks, w_out_att, w_o, final_norm_w):
    bp, tp, _ = x_prompt.shape
    bs, ts, _ = x_sample.shape
    xp = x_prompt.reshape(bp * tp, D_MODEL)
    xs = x_sample.reshape(bs * ts, D_MODEL)

    tri = jnp.tril(jnp.ones((SSM_CHUNK, SSM_CHUNK), BF16))
    expand = (jnp.arange(WIDTH)[None, :] // SSM_HEADDIM == jnp.arange(LANES)[:, None]).astype(BF16)
    consts = (jnp.tile(tri, (1, 3)), jnp.tile(expand, (2, 1)))
    to_win_minor = lambda c: jnp.transpose(c, (0, 1, 3, 4, 2))
    from_win_minor = lambda c: jnp.transpose(c, (0, 1, 4, 2, 3))
    w_t = jnp.swapaxes(w_in, 1, 2)
    s_states = (state_ssm.reshape(DEPTH, bs, WIDTH, SSM_STATE), state_conv_ssm, state_conv_cf,
                state_conv_sc, to_win_minor(cache_k), to_win_minor(cache_v))

    xnp = rmsnorm(xp, norm_w[0], EPS, BF16)
    xns = rmsnorm(xs, norm_w[0], EPS, BF16)
    p_prev = s_prev = None
    for l in range(DEPTH):
        lp = _prep_layer(l, norm_w, w_in, ssm_conv_w, ssm_conv_b, ssm_dt_bias, ssm_a_log, ssm_d,
                         ssm_norm_w, w_out_ssm, cf_conv_w, cf_conv_b, cf_ln_w, cf_ln_b, w_out_cf,
                         sc_conv_w, w_out_sc, att_sinks, w_out_att, w_o)
        next_w = norm_w[l + 1] if l + 1 < DEPTH else final_norm_w
        outp, p_prev = _layer(xp, xnp, w_t, lp, consts, l, None, p_prev, next_w, nbatch=bp, t=tp)
        outs, s_prev = _layer(xs, xns, w_t, lp, consts, l, s_states, s_prev, next_w, nbatch=bs, t=ts)
        if l + 1 < DEPTH:
            (xp, xnp), (xs, xns) = outp, outs
    y_prompt = outp[0].reshape(bp, tp, D_MODEL)
    y_sample = outs[0].reshape(bs, ts, D_MODEL)

    def unpack(st, nbatch):
        (h, conv_ssm), conv_cf, conv_sc, (kwin, vwin) = st
        return (h.reshape(DEPTH, nbatch, SSM_HEADS, SSM_HEADDIM, SSM_STATE), conv_ssm, conv_cf,
                conv_sc, from_win_minor(kwin), from_win_minor(vwin))

    return (y_prompt, y_sample, *unpack(p_prev, bp), *unpack(s_prev, bs))
```

```python
import functools

import jax
import jax.numpy as jnp
from jax import lax
from jax.experimental import pallas as pl
from jax.experimental.pallas import tpu as pltpu

F32 = jnp.float32
BF16 = jnp.bfloat16

D_MODEL = 2048
DEPTH = 2
N_BRANCH = 4
WIDTH = 1024
SSM_HEADS = 16
SSM_HEADDIM = 64
SSM_GROUPS = 4
SSM_STATE = 128
SSM_CONV = 4
SSM_CHUNK = 128
CF_CONV = 31
SC_CONV = 3
ATT_HEADS = 16
ATT_KV_HEADS = 4
ATT_HEAD_DIM = 64
ATT_KV_WIDTH = ATT_KV_HEADS * ATT_HEAD_DIM
ATT_REP = ATT_HEADS // ATT_KV_HEADS
WINDOW = 128
EPS = 1e-6

LANES = 128
SUBLANES = 8
VMEM_LIMIT = 56 * 1024 * 1024

COL_MERGE = 0
COL_Z = 8
COL_XS = 9
COL_BC = 10
COL_CF_A = 11
COL_CF_G = 12
COL_CF_GATE = 13
COL_SC_B = 14
COL_SC_C = 15
COL_SC_V = 16
COL_SC_GATE = 17
COL_Q = 18
COL_ATT_GATE = 19
COL_K = 80
COL_V = 81
PROJ_COLS = 20992


def _sigmoid(x):
    return 1.0 / (1.0 + jnp.exp(-x))


def _silu(x):
    return x * _sigmoid(x)


def _cparams(sem, limit=VMEM_LIMIT):
    return pltpu.CompilerParams(dimension_semantics=sem, vmem_limit_bytes=limit)


def _tile(m, target):
    t = min(m, target)
    while m % t or t % SUBLANES:
        t -= 1
    return t


def _const_spec(shape):
    return pl.BlockSpec(shape, lambda *_: (0,) * len(shape))


def _stacked_out(l, nbatch, nb, tail, prev):
    shape = (DEPTH, nbatch) + tail
    spec = pl.BlockSpec((1, nb) + tail, lambda b, *_: (l, b) + (0,) * len(tail))
    return jax.ShapeDtypeStruct(shape, F32), spec, ([] if prev is None else [prev])


def _rmsnorm_kernel(x_ref, w_ref, o_ref, *, eps):
    x = x_ref[...].astype(F32)
    ms = jnp.mean(x * x, axis=-1, keepdims=True)
    o_ref[...] = (x * lax.rsqrt(ms + eps) * w_ref[...]).astype(o_ref.dtype)


def rmsnorm(x, w, eps, out_dtype):
    m, d = x.shape
    tm = _tile(m, 512)
    return pl.pallas_call(
        functools.partial(_rmsnorm_kernel, eps=eps),
        grid=(m // tm,),
        in_specs=[pl.BlockSpec((tm, d), lambda i: (i, 0)), _const_spec((1, d))],
        out_specs=pl.BlockSpec((tm, d), lambda i: (i, 0)),
        out_shape=jax.ShapeDtypeStruct((m, d), out_dtype),
        compiler_params=_cparams(("parallel",)),
        name="rmsnorm",
    )(x, w.reshape(1, d))


_SEG_START = dict(z=0, xs=1024, bc=2048, dt=3072, cf_a=3088, cf_g=4112, cf_gate=5136, sc_b=6160,
                  sc_c=7184, sc_v=8208, sc_gate=9232, q=10256, k=11280, att_gate=11792, merge=12816)
PROJ_TN = 512


def _proj_block_starts():
    starts = [_SEG_START["merge"] + PROJ_TN * j for j in range(N_BRANCH * D_MODEL // PROJ_TN)]
    for name in ("z", "xs", "bc", "cf_a", "cf_g", "cf_gate", "sc_b", "sc_c", "sc_v", "sc_gate", "q",
                 "att_gate"):
        starts += [_SEG_START[name], _SEG_START[name] + PROJ_TN]
    starts.append(_SEG_START["k"])
    assert len(starts) * PROJ_TN == PROJ_COLS and all(v % SUBLANES == 0 for v in starts)
    return jnp.array([v // SUBLANES for v in starts], jnp.int32)


def _inproj_kernel(starts_ref, a_ref, w_ref, o_ref):
    del starts_ref
    w = w_ref[0].astype(BF16)
    o_ref[...] = lax.dot_general(a_ref[...], w, (((1,), (1,)), ((), ())),
                                 preferred_element_type=F32).astype(o_ref.dtype)


def in_proj(xn, w_t, l, tm):
    m, k = xn.shape
    tm = _tile(m, tm)
    grid_spec = pltpu.PrefetchScalarGridSpec(
        num_scalar_prefetch=1, grid=(m // tm, PROJ_COLS // PROJ_TN),
        in_specs=[pl.BlockSpec((tm, k), lambda i, j, s: (i, 0)),
                  pl.BlockSpec((pl.Element(1), pl.Element(PROJ_TN), pl.Element(k)),
                               lambda i, j, s: (l, s[j] * SUBLANES, 0))],
        out_specs=pl.BlockSpec((tm, PROJ_TN), lambda i, j, s: (i, j)))
    return pl.pallas_call(
        _inproj_kernel, grid_spec=grid_spec,
        out_shape=jax.ShapeDtypeStruct((m, PROJ_COLS), BF16),
        compiler_params=_cparams(("parallel", "arbitrary")),
        name="in_proj",
    )(_proj_block_starts(), xn, w_t)


def _out_kernel(h_ref, w_ref, x_ref, nw_ref, *out_refs, eps, last):
    x = x_ref[...] + jnp.dot(h_ref[...], w_ref[...], preferred_element_type=F32)
    ms = jnp.mean(x * x, axis=-1, keepdims=True)
    xn = x * lax.rsqrt(ms + eps) * nw_ref[...]
    if last:
        out_refs[0][...] = xn
    else:
        out_refs[0][...] = x
        out_refs[1][...] = xn.astype(out_refs[1].dtype)


def out_proj_norm(h, w, x, norm_w, eps, last, tm):
    m, k = h.shape
    n = w.shape[1]
    tm = _tile(m, tm)
    row = lambda i: (i, 0)
    if last:
        out_shape = [jax.ShapeDtypeStruct((m, n), F32)]
    else:
        out_shape = [jax.ShapeDtypeStruct((m, n), F32), jax.ShapeDtypeStruct((m, n), BF16)]
    return pl.pallas_call(
        functools.partial(_out_kernel, eps=eps, last=last),
        grid=(m // tm,),
        in_specs=[pl.BlockSpec((tm, k), row),
                  pl.BlockSpec((k, n), lambda i: (0, 0), pipeline_mode=pl.Buffered(1)),
                  pl.BlockSpec((tm, n), row), _const_spec((1, n))],
        out_specs=[pl.BlockSpec((tm, n), row)] * len(out_shape),
        out_shape=out_shape,
        compiler_params=_cparams(("parallel",)),
        name="out_proj_norm",
    )(h, w, x, norm_w.reshape(1, n))


def _merge_kernel(*refs):
    b_refs, w_refs, m_refs, o_ref = refs[0:4], refs[4:8], refs[8:12], refs[12]
    acc = None
    for b_ref, w_ref, m_ref in zip(b_refs, w_refs, m_refs):
        out = jnp.dot(b_ref[...], w_ref[0], preferred_element_type=F32)
        term = _sigmoid(m_ref[...].astype(F32)) * out
        acc = term if acc is None else acc + term
    o_ref[...] = acc.astype(o_ref.dtype)


def merge_branches(branches, w_outs, proj, l, tm, tn):
    m = proj.shape[0]
    tm = _tile(m, tm)
    nb = D_MODEL // tn
    br_spec = pl.BlockSpec((tm, WIDTH), lambda i, j: (i, 0))
    w_spec = pl.BlockSpec((1, WIDTH, tn), lambda i, j: (l, 0, j))
    merge_specs = [pl.BlockSpec((tm, tn), functools.partial(lambda i, j, b: (i, b * nb + j), b=b))
                   for b in range(N_BRANCH)]
    return pl.pallas_call(
        _merge_kernel,
        grid=(m // tm, nb),
        in_specs=[br_spec] * N_BRANCH + [w_spec] * N_BRANCH + merge_specs,
        out_specs=pl.BlockSpec((tm, tn), lambda i, j: (i, j)),
        out_shape=jax.ShapeDtypeStruct((m, D_MODEL), BF16),
        compiler_params=_cparams(("parallel", "arbitrary")),
        name="merge_branches",
    )(*branches, *w_outs, proj, proj, proj, proj)


def _pad_rows(x, rows):
    if x.shape[0] == rows:
        return x
    return jnp.concatenate([x, jnp.zeros((rows - x.shape[0], x.shape[1]), x.dtype)], axis=0)


def _bf16_terms(x, n):
    terms = []
    for _ in range(n):
        t = x.astype(BF16)
        terms.append(t)
        x = x - t.astype(F32)
    return terms


def _head_stats(dtr_all, dtb, alog, tri3, e2, *, L, nb):
    dtr_all = dtr_all + dtb
    dt_all = jnp.maximum(dtr_all, 0.0) + jnp.log1p(jnp.exp(-jnp.abs(dtr_all)))
    da_all = dt_all * (-jnp.exp(alog))
    css, ecss, decs = [], [], []
    for bi in range(nb):
        da = da_all[bi * L:(bi + 1) * L, :]
        if L == SSM_CHUNK:
            cs = jnp.dot(tri3, jnp.concatenate(_bf16_terms(da, 3), axis=0), preferred_element_type=F32)
        else:
            row = lax.broadcasted_iota(jnp.int32, (L, LANES), 0)
            cs = jnp.zeros((L, LANES), F32)
            for s in range(L):
                cs = cs + jnp.where(row >= s, da[s:s + 1, :], 0.0)
        css.append(cs)
        ecss.append(jnp.exp(cs))
        decs.append(jnp.exp(cs[L - 1:L, :] - cs))
    stack = jnp.concatenate([dt_all] + ecss + decs, axis=0)
    ex = jnp.dot(jnp.concatenate(_bf16_terms(stack, 2), axis=1), e2, preferred_element_type=F32)
    n = nb * L
    return [(css[bi], ex[bi * L:(bi + 1) * L], ex[n + bi * L:n + (bi + 1) * L],
             ex[2 * n + bi * L:2 * n + (bi + 1) * L]) for bi in range(nb)]


def _ssd_core(xbc, z, stats, h_ref, dsk, nw, *, L):
    cs, dt_e, ecs_e, dec_e = stats
    mm = BF16 if L >= 16 else F32
    xs = xbc[:, 0:WIDTH]
    bmat = xbc[:, WIDTH:WIDTH + SSM_GROUPS * SSM_STATE]
    cmat = xbc[:, WIDTH + SSM_GROUPS * SSM_STATE:]
    xdt = xs * dt_e
    xdd = xdt * dec_e

    lp = SSM_CHUNK
    xdt_p = _pad_rows(xdt, lp).astype(mm)
    xdd_p = _pad_rows(xdd, lp)
    b_p = _pad_rows(bmat, lp).astype(mm)
    cs_t = _pad_rows(cs, lp).T
    cmat_m = cmat.astype(mm)

    li = lax.broadcasted_iota(jnp.int32, (L, lp), 0)
    sj = lax.broadcasted_iota(jnp.int32, (L, lp), 1)
    causal = sj <= li
    lane = lax.broadcasted_iota(jnp.int32, (lp, LANES), 1)
    lo_half = lane < SSM_HEADDIM

    y_blocks = []
    hpg = SSM_HEADS // SSM_GROUPS
    gw = hpg * SSM_HEADDIM
    bpg = gw // LANES
    for g in range(SSM_GROUPS):
        c_g = cmat_m[:, g * SSM_STATE:(g + 1) * SSM_STATE]
        b_g = b_p[:, g * SSM_STATE:(g + 1) * SSM_STATE]
        cb = lax.dot_general(c_g, b_g, (((1,), (1,)), ((), ())), preferred_element_type=F32)
        h_blk = h_ref[g * gw:(g + 1) * gw, :]
        y_off = lax.dot_general(c_g, h_blk.astype(mm), (((1,), (1,)), ((), ())),
                                preferred_element_type=F32)
        y_off = y_off * ecs_e[:, g * gw:(g + 1) * gw]
        for jb in range(bpg):
            blk = g * bpg + jb
            x_blk = xdt_p[:, blk * LANES:(blk + 1) * LANES]
            acc = None
            for par in range(2):
                h = 2 * blk + par
                diff = cs[:, h:h + 1] - cs_t[h:h + 1, :]
                lm = jnp.exp(jnp.where(causal, diff, -jnp.inf))
                gmat = (cb * lm).astype(mm)
                x_half = jnp.where(lo_half if par == 0 else jnp.logical_not(lo_half), x_blk,
                                   jnp.zeros_like(x_blk))
                t = jnp.dot(gmat, x_half, preferred_element_type=F32)
                acc = t if acc is None else acc + t
            y_blocks.append(acc + y_off[:, jb * LANES:(jb + 1) * LANES])
        xdd_t = jnp.concatenate(
            [xdd_p[:, (g * bpg + jb) * LANES:(g * bpg + jb + 1) * LANES].T for jb in range(bpg)],
            axis=0).astype(mm)
        s_new = jnp.dot(xdd_t, b_g, preferred_element_type=F32)
        for r in range(hpg):
            h = g * hpg + r
            scal = jnp.exp(cs_t[h:h + 1, L - 1:L])
            rows = slice(h * SSM_HEADDIM, (h + 1) * SSM_HEADDIM)
            h_ref[rows, :] = h_ref[rows, :] * scal + s_new[r * SSM_HEADDIM:(r + 1) * SSM_HEADDIM, :]

    y = jnp.concatenate(y_blocks, axis=1) + dsk * xs
    y = y * _silu(z)
    ms = jnp.mean(y * y, axis=-1, keepdims=True)
    return y * lax.rsqrt(ms + 1e-5) * nw


def _ssd_kernel(*refs, L, nc, nb, zero_state, n_alias):
    xs_ref, bc_ref, z_ref, xn_ref, wdt_ref = refs[:5]
    refs = refs[5:]
    if not zero_state:
        pre_ref, h0_ref = refs[:2]
        refs = refs[2:]
    cw_ref, cb_ref, dtb_ref, alog_ref, dsk_ref, nw_ref, tri_ref, e_ref = refs[:8]
    y_ref, hout_ref, cst_ref, xp_ref, h_ref = refs[8 + n_alias:]
    c = pl.program_id(1)
    off = SUBLANES - (SSM_CONV - 1)

    dtr_all = lax.dot_general(xn_ref[...], wdt_ref[0].astype(BF16), (((1,), (1,)), ((), ())),
                              preferred_element_type=F32)
    stats = _head_stats(dtr_all, dtb_ref[...], alog_ref[...], tri_ref[...], e_ref[...], L=L, nb=nb)
    xs_all = xs_ref[...].astype(F32)
    bc_all = bc_ref[...].astype(F32)
    z_all = z_ref[...].astype(F32)
    ys = []
    for bi in range(nb):
        rows = slice(bi * L, (bi + 1) * L)

        @pl.when(c == 0)
        def _():
            if zero_state:
                xp_ref[bi, off:SUBLANES, :] = jnp.zeros((SSM_CONV - 1, 2 * WIDTH), F32)
                h_ref[bi] = jnp.zeros((WIDTH, SSM_STATE), F32)
            else:
                xp_ref[bi, off:SUBLANES, :] = pre_ref[0, bi]
                h_ref[bi] = h0_ref[0, bi]

        if nc > 1:
            @pl.when(c > 0)
            def _():
                xp_ref[bi, off:SUBLANES, :] = xp_ref[bi, L + off:L + SUBLANES, :]

        xp_ref[bi, SUBLANES:SUBLANES + L, 0:WIDTH] = xs_all[rows, :]
        xp_ref[bi, SUBLANES:SUBLANES + L, WIDTH:2 * WIDTH] = bc_all[rows, :]

        conv = cb_ref[...] + cw_ref[0:1, :] * xp_ref[bi, off:off + L, :]
        for k in range(1, SSM_CONV):
            conv = conv + cw_ref[k:k + 1, :] * xp_ref[bi, off + k:off + k + L, :]
        ys.append(_ssd_core(_silu(conv), z_all[rows, :], stats[bi], h_ref.at[bi], dsk_ref[...],
                            nw_ref[...], L=L))

        @pl.when(c == nc - 1)
        def _():
            hout_ref[0, bi] = h_ref[bi]
            cst_ref[0, bi] = xp_ref[bi, L + off:L + SUBLANES, :]

    y_ref[...] = jnp.concatenate(ys, axis=0).astype(y_ref.dtype)


def ssd_branch(proj, xn, w_t, lp, consts, l, *, nbatch, t, conv_state=None, ssm_state=None, prev=None):
    L = SSM_CHUNK if t % SSM_CHUNK == 0 else t
    nc = t // L
    zero_state = conv_state is None
    nb = 4 if nc == 1 and nbatch % 4 == 0 else 1
    m = nbatch * t
    rl = nb * L

    def rows(b, c):
        return b * nc + c

    def col(cidx):
        return pl.BlockSpec((rl, WIDTH), lambda b, c: (rows(b, c), cidx))

    tri, e = consts
    in_specs = [col(COL_XS), col(COL_BC), col(COL_Z),
                pl.BlockSpec((rl, D_MODEL), lambda b, c: (rows(b, c), 0)),
                pl.BlockSpec((1, LANES, D_MODEL), lambda b, c: (l, _SEG_START["dt"] // LANES, 0))]
    args = [proj, proj, proj, xn, w_t]
    if not zero_state:
        in_specs += [pl.BlockSpec((1, nb, SSM_CONV - 1, 2 * WIDTH), lambda b, c: (l, b, 0, 0)),
                     pl.BlockSpec((1, nb, WIDTH, SSM_STATE), lambda b, c: (l, b, 0, 0))]
        args += [conv_state, ssm_state]
    in_specs += [_const_spec((SSM_CONV, 2 * WIDTH)), _const_spec((1, 2 * WIDTH)),
                 _const_spec((1, LANES)), _const_spec((1, LANES)), _const_spec((1, WIDTH)),
                 _const_spec((1, WIDTH)), _const_spec((SSM_CHUNK, 3 * SSM_CHUNK)),
                 _const_spec((2 * LANES, WIDTH))]
    args += [lp["ssm_conv_w"], lp["ssm_conv_b"], lp["dt_bias"], lp["a_log"], lp["d_skip"],
             lp["ssm_norm_w"], tri, e]
    h_shape, h_spec, h_prev = _stacked_out(l, nbatch, nb, (WIDTH, SSM_STATE), prev and prev[0])
    c_shape, c_spec, c_prev = _stacked_out(l, nbatch, nb, (SSM_CONV - 1, 2 * WIDTH), prev and prev[1])
    alias_in = h_prev + c_prev
    aliases = {len(args) + i: 1 + i for i in range(len(alias_in))}
    in_specs += [pl.BlockSpec(memory_space=pl.ANY)] * len(alias_in)
    y, h_out, c_out = pl.pallas_call(
        functools.partial(_ssd_kernel, L=L, nc=nc, nb=nb, zero_state=zero_state, n_alias=len(alias_in)),
        grid=(nbatch // nb, nc),
        in_specs=in_specs,
        out_specs=[pl.BlockSpec((rl, WIDTH), lambda b, c: (rows(b, c), 0)), h_spec, c_spec],
        out_shape=[jax.ShapeDtypeStruct((m, WIDTH), BF16), h_shape, c_shape],
        scratch_shapes=[pltpu.VMEM((nb, SUBLANES + L, 2 * WIDTH), F32),
                        pltpu.VMEM((nb, WIDTH, SSM_STATE), F32)],
        input_output_aliases=aliases,
        compiler_params=_cparams(("parallel", "arbitrary")),
        name="ssd_branch",
    )(*args, *alias_in)
    return y, (h_out, c_out)


def _dwconv(xp_ref, xsh_ref, w_ref, b_ref, acc_ref, *, k, start, tb, row0):
    rbk = min(tb, 32)
    if xsh_ref is not None:
        nrows = xsh_ref.shape[1]
        for s in range(1, SUBLANES):
            xsh_ref[s - 1] = xp_ref[s:s + nrows, :]
    for j in range(WIDTH // LANES):
        lanes = slice(j * LANES, (j + 1) * LANES)
        for r in range(tb // rbk):
            r0 = r * rbk
            acc = None
            for i in range(k):
                first = start + i + r0
                s = first % SUBLANES
                if xsh_ref is None or s == 0:
                    x = xp_ref[first:first + rbk, lanes]
                else:
                    x = xsh_ref[s - 1, first - s:first - s + rbk, lanes]
                term = w_ref[i:i + 1, lanes] * x
                if acc is None:
                    acc = term if b_ref is None else term + b_ref[:, lanes]
                else:
                    acc = acc + term
            acc_ref[row0 + r0:row0 + r0 + rbk, lanes] = acc


def _conv_all(u, xp_ref, pre_ref, st_ref, w_ref, b_ref, acc_ref, xsh_ref=None, *, k, p, tb, nc, nb):
    c = pl.program_id(1)
    for bi in range(nb):
        @pl.when(c == 0)
        def _():
            if pre_ref is None:
                xp_ref[bi, p - (k - 1):p, :] = jnp.zeros((k - 1, WIDTH), F32)
            else:
                xp_ref[bi, p - (k - 1):p, :] = pre_ref[0, bi]

        if nc > 1:
            @pl.when(c > 0)
            def _():
                xp_ref[bi, 0:p, :] = xp_ref[bi, tb:tb + p, :]

        xp_ref[bi, p:p + tb, :] = u[bi * tb:(bi + 1) * tb, :]
        _dwconv(xp_ref.at[bi], xsh_ref, w_ref, b_ref, acc_ref, k=k, start=p - (k - 1), tb=tb,
                row0=bi * tb)

        @pl.when(c == nc - 1)
        def _():
            st_ref[0, bi] = xp_ref[bi, p + tb - (k - 1):p + tb, :]


def _cf_kernel(*refs, tb, nc, nb, p, zero_state, n_alias):
    a_ref, g_ref, gate_ref = refs[:3]
    refs = refs[3:]
    pre_ref = None
    if not zero_state:
        pre_ref, refs = refs[0], refs[1:]
    w_ref, b_ref, lnw_ref, lnb_ref = refs[:4]
    o_ref, st_ref, xp_ref, acc_ref, *xsh = refs[4 + n_alias:]
    u = a_ref[...].astype(F32) * _sigmoid(g_ref[...].astype(F32))
    _conv_all(u, xp_ref, pre_ref, st_ref, w_ref, b_ref, acc_ref, *xsh, k=CF_CONV, p=p, tb=tb, nc=nc,
              nb=nb)
    u = acc_ref[...]
    mu = jnp.mean(u, axis=-1, keepdims=True)
    uc = u - mu
    var = jnp.mean(uc * uc, axis=-1, keepdims=True)
    u = _silu(uc * lax.rsqrt(var + 1e-5) * lnw_ref[...] + lnb_ref[...])
    o_ref[...] = (u * _silu(gate_ref[...].astype(F32))).astype(o_ref.dtype)


def _sc_kernel(*refs, tb, nc, nb, p, zero_state, n_alias):
    gb_ref, gc_ref, sv_ref, gate_ref = refs[:4]
    refs = refs[4:]
    pre_ref = None
    if not zero_state:
        pre_ref, refs = refs[0], refs[1:]
    w_ref = refs[0]
    o_ref, st_ref, xp_ref, acc_ref = refs[1 + n_alias:]
    u = gc_ref[...].astype(F32) * sv_ref[...].astype(F32)
    _conv_all(u, xp_ref, pre_ref, st_ref, w_ref, None, acc_ref, k=SC_CONV, p=p, tb=tb, nc=nc, nb=nb)
    o_ref[...] = (gb_ref[...].astype(F32) * acc_ref[...]
                  * _silu(gate_ref[...].astype(F32))).astype(o_ref.dtype)


def conv_branch(kind, proj, lp, l, *, nbatch, t, state=None, prev=None):
    tb = min(t, 256)
    nc = t // tb
    nb = 8 if nc == 1 and nbatch % 8 == 0 else 1
    m = nbatch * t
    rl = nb * tb
    zero_state = state is None
    k = CF_CONV if kind == "cf" else SC_CONV
    p = -(-(k - 1) // SUBLANES) * SUBLANES

    def rows(b, c):
        return b * nc + c

    def col(cidx):
        return pl.BlockSpec((rl, WIDTH), lambda b, c: (rows(b, c), cidx))

    if kind == "cf":
        in_specs = [col(COL_CF_A), col(COL_CF_G), col(COL_CF_GATE)]
        args = [proj, proj, proj]
        w_specs = [_const_spec((k, WIDTH)), _const_spec((1, WIDTH)), _const_spec((1, WIDTH)),
                   _const_spec((1, WIDTH))]
        w_args = [lp["cf_conv_w"], lp["cf_conv_b"], lp["cf_ln_w"], lp["cf_ln_b"]]
        body = _cf_kernel
    else:
        in_specs = [col(COL_SC_B), col(COL_SC_C), col(COL_SC_V), col(COL_SC_GATE)]
        args = [proj, proj, proj, proj]
        w_specs = [_const_spec((k, WIDTH))]
        w_args = [lp["sc_conv_w"]]
        body = _sc_kernel
    if not zero_state:
        in_specs.append(pl.BlockSpec((1, nb, k - 1, WIDTH), lambda b, c: (l, b, 0, 0)))
        args.append(state)
    in_specs += w_specs
    args += w_args
    s_shape, s_spec, alias_in = _stacked_out(l, nbatch, nb, (k - 1, WIDTH), prev)
    aliases = {len(args) + i: 1 + i for i in range(len(alias_in))}
    in_specs += [pl.BlockSpec(memory_space=pl.ANY)] * len(alias_in)
    scratch = [pltpu.VMEM((nb, p + tb, WIDTH), F32), pltpu.VMEM((rl, WIDTH), F32)]
    if kind == "cf" and nb == 1 and tb % 32 == 0:
        scratch.append(pltpu.VMEM((SUBLANES - 1, p + tb - SUBLANES, WIDTH), F32))
    return pl.pallas_call(
        functools.partial(body, tb=tb, nc=nc, nb=nb, p=p, zero_state=zero_state, n_alias=len(alias_in)),
        grid=(nbatch // nb, nc),
        in_specs=in_specs,
        out_specs=[pl.BlockSpec((rl, WIDTH), lambda b, c: (rows(b, c), 0)), s_spec],
        out_shape=[jax.ShapeDtypeStruct((m, WIDTH), BF16), s_shape],
        scratch_shapes=scratch,
        input_output_aliases=aliases,
        compiler_params=_cparams(("parallel", "arbitrary")),
        name=kind + "_branch",
    )(*args, *alias_in)


def _sink_column(sink_ref, g, rows_per_head):
    n = ATT_REP * rows_per_head
    rid = lax.broadcasted_iota(jnp.int32, (n, 1), 0)
    col = jnp.full((n, 1), sink_ref[g * ATT_REP + ATT_REP - 1], F32)
    for r in range(ATT_REP - 2, -1, -1):
        col = jnp.where(rid < (r + 1) * rows_per_head, sink_ref[g * ATT_REP + r], col)
    return col


def _softmax_sink(s, ok, sink_col):
    s = jnp.where(ok, s, -jnp.inf)
    m = jnp.maximum(jnp.max(s, axis=-1, keepdims=True), sink_col)
    p = jnp.exp(s - m)
    denom = jnp.sum(p, axis=-1, keepdims=True) + jnp.exp(sink_col - m)
    return p / denom


def _attn_prompt_kernel(q_ref, kp_ref, kc_ref, vp_ref, vc_ref, gate_ref, sink_ref, *rest, nblk, n_alias):
    o_ref, kw_ref, vw_ref = rest[n_alias:]
    L = WINDOW
    i = pl.program_id(1)
    q = q_ref[...].astype(F32) * (ATT_HEAD_DIM ** -0.5)
    k2 = jnp.concatenate([kp_ref[...], kc_ref[...]], axis=0).astype(F32)
    v2 = jnp.concatenate([vp_ref[...], vc_ref[...]], axis=0).astype(F32)
    li = lax.broadcasted_iota(jnp.int32, (ATT_REP * L, 2 * L), 0) & (L - 1)
    sj = lax.broadcasted_iota(jnp.int32, (ATT_REP * L, 2 * L), 1)
    ok = (sj > li) & (sj <= li + WINDOW) & ((sj >= L) | (i > 0))
    lane = lax.broadcasted_iota(jnp.int32, (2 * L, LANES), 1)
    lo_half = lane < ATT_HEAD_DIM

    o_blocks = [None] * (WIDTH // LANES)
    for g in range(ATT_KV_HEADS):
        half = g % 2
        sl = slice((g // 2) * LANES, (g // 2 + 1) * LANES)
        keep = lo_half if half == 0 else jnp.logical_not(lo_half)
        k_use = jnp.where(keep, k2[:, sl], 0.0).astype(BF16)
        v_use = jnp.where(keep, v2[:, sl], 0.0).astype(BF16)
        parts = []
        for r in range(ATT_REP):
            h = g * ATT_REP + r
            qb = q[:, (h // 2) * LANES:(h // 2 + 1) * LANES]
            parts.append(qb if h % 2 == half else pltpu.roll(qb, ATT_HEAD_DIM, 1))
        lhs = jnp.concatenate(parts, axis=0).astype(BF16)
        s = lax.dot_general(lhs, k_use, (((1,), (1,)), ((), ())), preferred_element_type=F32)
        pn = _softmax_sink(s, ok, _sink_column(sink_ref, g, L)).astype(BF16)
        o = jnp.dot(pn, v_use, preferred_element_type=F32)
        for r in range(ATT_REP):
            h = g * ATT_REP + r
            ob = o[r * L:(r + 1) * L, :]
            if h % 2 != half:
                ob = pltpu.roll(ob, ATT_HEAD_DIM, 1)
            o_blocks[h // 2] = ob if o_blocks[h // 2] is None else o_blocks[h // 2] + ob
    o = jnp.concatenate(o_blocks, axis=1)
    o_ref[...] = (o * _silu(gate_ref[...].astype(F32))).astype(o_ref.dtype)

    @pl.when(i == nblk - 1)
    def _():
        kw_ref[0, 0] = kc_ref[...].astype(F32).T.reshape(ATT_KV_HEADS, ATT_HEAD_DIM, WINDOW)
        vw_ref[0, 0] = vc_ref[...].astype(F32).T.reshape(ATT_KV_HEADS, ATT_HEAD_DIM, WINDOW)


def _win_out(l, nbatch, nb, prev):
    return _stacked_out(l, nbatch, nb, (ATT_KV_HEADS, ATT_HEAD_DIM, WINDOW), prev)


def attn_prompt(proj, sinks, l, *, nbatch, t, prev=None):
    L = WINDOW
    nblk = t // L

    def col(cidx):
        return pl.BlockSpec((L, WIDTH), lambda b, i: (b * nblk + i, cidx))

    def kv(cidx, back):
        return pl.BlockSpec((L, ATT_KV_WIDTH), lambda b, i: (b * nblk + jnp.maximum(i - back, 0), cidx))

    k_shape, k_spec, k_prev = _win_out(l, nbatch, 1, prev and prev[0])
    v_shape, v_spec, v_prev = _win_out(l, nbatch, 1, prev and prev[1])
    alias_in = k_prev + v_prev
    args = [proj, proj, proj, proj, proj, proj, sinks]
    aliases = {len(args) + i: 1 + i for i in range(len(alias_in))}
    y, kw, vw = pl.pallas_call(
        functools.partial(_attn_prompt_kernel, nblk=nblk, n_alias=len(alias_in)),
        grid=(nbatch, nblk),
        in_specs=[col(COL_Q), kv(COL_K, 1), kv(COL_K, 0), kv(COL_V, 1), kv(COL_V, 0),
                  col(COL_ATT_GATE), pl.BlockSpec(memory_space=pltpu.SMEM)]
        + [pl.BlockSpec(memory_space=pl.ANY)] * len(alias_in),
        out_specs=[pl.BlockSpec((L, WIDTH), lambda b, i: (b * nblk + i, 0)), k_spec, v_spec],
        out_shape=[jax.ShapeDtypeStruct((nbatch * t, WIDTH), BF16), k_shape, v_shape],
        input_output_aliases=aliases,
        compiler_params=_cparams(("parallel", "arbitrary")),
        name="attn_prompt",
    )(*args, *alias_in)
    return y, (kw, vw)


def _attn_sample_kernel(q_ref, kn_ref, vn_ref, ck_ref, cv_ref, gate_ref, sink_ref, *rest, L, nb, n_alias):
    o_ref, kw_ref, vw_ref = rest[n_alias:]
    q = q_ref[...].astype(F32) * (ATT_HEAD_DIM ** -0.5)
    kn_t = _pad_rows(kn_ref[...].astype(F32), WINDOW).T
    vn_t = _pad_rows(vn_ref[...].astype(F32), WINDOW).T
    nkeys = 2 * WINDOW
    rq = ATT_REP * L
    row = lax.broadcasted_iota(jnp.int32, (nb * rq, nkeys), 0)
    sj = lax.broadcasted_iota(jnp.int32, (nb * rq, nkeys), 1)
    li = row & (L - 1)
    cj = sj - WINDOW - (row // rq) * L
    ok = ((sj < WINDOW) & (sj > li)) | ((cj >= 0) & (cj <= li))
    head_of_row = (lax.broadcasted_iota(jnp.int32, (nb * rq, 1), 0) // L) % ATT_REP
    lane = lax.broadcasted_iota(jnp.int32, (ATT_HEAD_DIM, WINDOW), 1)
    is_new = lane >= WINDOW - L
    pieces = [[None] * ATT_HEADS for _ in range(nb)]
    for g in range(ATT_KV_HEADS):
        gl = slice(g * ATT_HEAD_DIM, (g + 1) * ATT_HEAD_DIM)
        kn_g, vn_g = kn_t[gl, :], vn_t[gl, :]
        scores, values = [], []
        for bi in range(nb):
            rows = slice(bi * L, (bi + 1) * L)
            kc, vc = ck_ref[0, bi, g], cv_ref[0, bi, g]
            k2 = jnp.concatenate([kc, kn_g], axis=1).astype(BF16)
            values.append(jnp.concatenate([vc, vn_g], axis=1).astype(BF16))
            lhs = jnp.concatenate(
                [q[rows, (g * ATT_REP + r) * ATT_HEAD_DIM:(g * ATT_REP + r + 1) * ATT_HEAD_DIM]
                 for r in range(ATT_REP)], axis=0).astype(BF16)
            scores.append(jnp.dot(lhs, k2, preferred_element_type=F32))
            shift_new = (WINDOW - L - bi * L) % WINDOW
            for c_old, n_g, w_ref in ((kc, kn_g, kw_ref), (vc, vn_g, vw_ref)):
                tail = n_g if shift_new == 0 else pltpu.roll(n_g, shift_new, 1)
                w_ref[0, bi, g] = jnp.where(is_new, tail, pltpu.roll(c_old, WINDOW - L, 1))
        sink_col = jnp.full((nb * rq, 1), sink_ref[g * ATT_REP], F32)
        for r in range(1, ATT_REP):
            sink_col = jnp.where(head_of_row == r, sink_ref[g * ATT_REP + r], sink_col)
        pn = _softmax_sink(jnp.concatenate(scores, axis=0), ok, sink_col).astype(BF16)
        for bi in range(nb):
            o = lax.dot_general(pn[bi * rq:(bi + 1) * rq, :], values[bi], (((1,), (1,)), ((), ())),
                                preferred_element_type=F32)
            for r in range(ATT_REP):
                pieces[bi][g * ATT_REP + r] = o[r * L:(r + 1) * L, :]
    o = jnp.concatenate([jnp.concatenate(p, axis=1) for p in pieces], axis=0)
    o_ref[...] = (o * _silu(gate_ref[...].astype(F32))).astype(o_ref.dtype)


def attn_sample(proj, sinks, cache_k, cache_v, l, *, nbatch, t, prev=None):
    nb = WINDOW // t if nbatch % (WINDOW // t) == 0 else nbatch
    rl = nb * t

    def col(cidx, w):
        return pl.BlockSpec((rl, w), lambda b: (b, cidx))

    cache = pl.BlockSpec((1, nb, ATT_KV_HEADS, ATT_HEAD_DIM, WINDOW), lambda b: (l, b, 0, 0, 0))
    k_shape, k_spec, k_prev = _win_out(l, nbatch, nb, prev and prev[0])
    v_shape, v_spec, v_prev = _win_out(l, nbatch, nb, prev and prev[1])
    alias_in = k_prev + v_prev
    args = [proj, proj, proj, cache_k, cache_v, proj, sinks]
    aliases = {len(args) + i: 1 + i for i in range(len(alias_in))}
    y, kw, vw = pl.pallas_call(
        functools.partial(_attn_sample_kernel, L=t, nb=nb, n_alias=len(alias_in)),
        grid=(nbatch // nb,),
        in_specs=[col(COL_Q, WIDTH), col(COL_K, ATT_KV_WIDTH), col(COL_V, ATT_KV_WIDTH), cache, cache,
                  col(COL_ATT_GATE, WIDTH), pl.BlockSpec(memory_space=pltpu.SMEM)]
        + [pl.BlockSpec(memory_space=pl.ANY)] * len(alias_in),
        out_specs=[pl.BlockSpec((rl, WIDTH), lambda b: (b, 0)), k_spec, v_spec],
        out_shape=[jax.ShapeDtypeStruct((nbatch * t, WIDTH), BF16), k_shape, v_shape],
        input_output_aliases=aliases,
        compiler_params=_cparams(("parallel",)),
        name="attn_sample",
    )(*args, *alias_in)
    return y, (kw, vw)


def _prep_layer(l, norm_w, w_in, ssm_conv_w, ssm_conv_b, ssm_dt_bias, ssm_a_log, ssm_d, ssm_norm_w,
                w_out_ssm, cf_conv_w, cf_conv_b, cf_ln_w, cf_ln_b, w_out_cf, sc_conv_w, w_out_sc,
                att_sinks, w_out_att, w_o):
    del w_in
    pad_h = (0, LANES - SSM_HEADS)
    return dict(
        norm_w=norm_w[l],
        ssm_conv_w=ssm_conv_w[l], ssm_conv_b=ssm_conv_b[l].reshape(1, -1),
        dt_bias=jnp.pad(ssm_dt_bias[l], pad_h).reshape(1, LANES),
        a_log=jnp.pad(ssm_a_log[l], pad_h).reshape(1, LANES),
        d_skip=jnp.repeat(ssm_d[l], SSM_HEADDIM).reshape(1, WIDTH),
        ssm_norm_w=ssm_norm_w[l].reshape(1, WIDTH),
        cf_conv_w=cf_conv_w[l], cf_conv_b=cf_conv_b[l].reshape(1, WIDTH),
        cf_ln_w=cf_ln_w[l].reshape(1, WIDTH), cf_ln_b=cf_ln_b[l].reshape(1, WIDTH),
        sc_conv_w=sc_conv_w[l], sinks=att_sinks[l],
        w_outs=tuple(w.astype(BF16) for w in (w_out_ssm, w_out_cf, w_out_sc, w_out_att)),
        w_o=w_o[l].astype(BF16),
    )


def _layer(x, xn, w_t, lp, consts, l, states, prev, next_norm_w, *, nbatch, t):
    kw = dict(nbatch=nbatch, t=t)
    proj = in_proj(xn, w_t, l, tm=2048)
    if states is None:
        ya, st_a = ssd_branch(proj, xn, w_t, lp, consts, l, prev=prev and prev[0], **kw)
        yb, st_b = conv_branch("cf", proj, lp, l, prev=prev and prev[1], **kw)
        yc, st_c = conv_branch("sc", proj, lp, l, prev=prev and prev[2], **kw)
        yd, st_d = attn_prompt(proj, lp["sinks"], l, prev=prev and prev[3], **kw)
    else:
        ssm, conv_ssm, conv_cf, conv_sc, cache_k, cache_v = states
        ya, st_a = ssd_branch(proj, xn, w_t, lp, consts, l, conv_state=conv_ssm, ssm_state=ssm,
                              prev=prev and prev[0], **kw)
        yb, st_b = conv_branch("cf", proj, lp, l, state=conv_cf, prev=prev and prev[1], **kw)
        yc, st_c = conv_branch("sc", proj, lp, l, state=conv_sc, prev=prev and prev[2], **kw)
        yd, st_d = attn_sample(proj, lp["sinks"], cache_k, cache_v, l, prev=prev and prev[3], **kw)
    hmix = merge_branches((ya, yb, yc, yd), lp["w_outs"], proj, l, tm=1024, tn=512)
    outs = out_proj_norm(hmix, lp["w_o"], x, next_norm_w, EPS, last=l == DEPTH - 1, tm=512)
    return outs, (st_a, st_b, st_c, st_d)


def kernel(x_prompt, x_sample, state_ssm, state_conv_ssm, state_conv_cf, state_conv_sc, cache_k, cache_v, norm_w, w_in, ssm_conv_w, ssm_conv_b, ssm_dt_bias, ssm_a_log, ssm_d, ssm_norm_w, w_out_ssm, cf_conv_w, cf_conv_b, cf_ln_w, cf_ln_b, w_out_cf, sc_conv_w, w_out_sc, att_sinks, w_out_att, w_o, final_norm_w):
    bp, tp, _ = x_prompt.shape
    bs, ts, _ = x_sample.shape
    xp = x_prompt.reshape(bp * tp, D_MODEL)
    xs = x_sample.reshape(bs * ts, D_MODEL)

    tri = jnp.tril(jnp.ones((SSM_CHUNK, SSM_CHUNK), BF16))
    expand = (jnp.arange(WIDTH)[None, :] // SSM_HEADDIM == jnp.arange(LANES)[:, None]).astype(BF16)
    consts = (jnp.tile(tri, (1, 3)), jnp.tile(expand, (2, 1)))
    to_win_minor = lambda c: jnp.transpose(c, (0, 1, 3, 4, 2))
    from_win_minor = lambda c: jnp.transpose(c, (0, 1, 4, 2, 3))
    w_t = jnp.swapaxes(w_in, 1, 2)
    s_states = (state_ssm.reshape(DEPTH, bs, WIDTH, SSM_STATE), state_conv_ssm, state_conv_cf,
                state_conv_sc, to_win_minor(cache_k), to_win_minor(cache_v))

    xnp = rmsnorm(xp, norm_w[0], EPS, BF16)
    xns = rmsnorm(xs, norm_w[0], EPS, BF16)
    p_prev = s_prev = None
    for l in range(DEPTH):
        lp = _prep_layer(l, norm_w, w_in, ssm_conv_w, ssm_conv_b, ssm_dt_bias, ssm_a_log, ssm_d,
                         ssm_norm_w, w_out_ssm, cf_conv_w, cf_conv_b, cf_ln_w, cf_ln_b, w_out_cf,
                         sc_conv_w, w_out_sc, att_sinks, w_out_att, w_o)
        next_w = norm_w[l + 1] if l + 1 < DEPTH else final_norm_w
        outp, p_prev = _layer(xp, xnp, w_t, lp, consts, l, None, p_prev, next_w, nbatch=bp, t=tp)
        outs, s_prev = _layer(xs, xns, w_t, lp, consts, l, s_states, s_prev, next_w, nbatch=bs, t=ts)
        if l + 1 < DEPTH:
            (xp, xnp), (xs, xns) = outp, outs
    y_prompt = outp[0].reshape(bp, tp, D_MODEL)
    y_sample = outs[0].reshape(bs, ts, D_MODEL)

    def unpack(st, nbatch):
        (h, conv_ssm), conv_cf, conv_sc, (kwin, vwin) = st
        return (h.reshape(DEPTH, nbatch, SSM_HEADS, SSM_HEADDIM, SSM_STATE), conv_ssm, conv_cf,
                conv_sc, from_win_minor(kwin), from_win_minor(vwin))

    return (y_prompt, y_sample, *unpack(p_prev, bp), *unpack(s_prev, bs))
```

```python
import functools

import jax
import jax.numpy as jnp
from jax import lax
from jax.experimental import pallas as pl
from jax.experimental.pallas import tpu as pltpu

F32 = jnp.float32
BF16 = jnp.bfloat16

D_MODEL = 2048
DEPTH = 2
N_BRANCH = 4
WIDTH = 1024
SSM_HEADS = 16
SSM_HEADDIM = 64
SSM_GROUPS = 4
SSM_STATE = 128
SSM_CONV = 4
SSM_CHUNK = 128
CF_CONV = 31
SC_CONV = 3
ATT_HEADS = 16
ATT_KV_HEADS = 4
ATT_HEAD_DIM = 64
ATT_KV_WIDTH = ATT_KV_HEADS * ATT_HEAD_DIM
ATT_REP = ATT_HEADS // ATT_KV_HEADS
WINDOW = 128
EPS = 1e-6

LANES = 128
SUBLANES = 8
VMEM_LIMIT = 56 * 1024 * 1024

COL_MERGE = 0
COL_Z = 8
COL_XS = 9
COL_BC = 10
COL_CF_A = 11
COL_CF_G = 12
COL_CF_GATE = 13
COL_SC_B = 14
COL_SC_C = 15
COL_SC_V = 16
COL_SC_GATE = 17
COL_Q = 18
COL_ATT_GATE = 19
COL_K = 80
COL_V = 81
PROJ_COLS = 21504


def _sigmoid(x):
    return 1.0 / (1.0 + jnp.exp(-x))


def _silu(x):
    return x * _sigmoid(x)


def _cparams(sem, limit=VMEM_LIMIT):
    return pltpu.CompilerParams(dimension_semantics=sem, vmem_limit_bytes=limit)


def _tile(m, target):
    t = min(m, target)
    while m % t or t % SUBLANES:
        t -= 1
    return t


def _const_spec(shape):
    return pl.BlockSpec(shape, lambda *_: (0,) * len(shape))


def _stacked_out(l, nbatch, nb, tail, prev):
    shape = (DEPTH, nbatch) + tail
    spec = pl.BlockSpec((1, nb) + tail, lambda b, *_: (l, b) + (0,) * len(tail))
    return jax.ShapeDtypeStruct(shape, F32), spec, ([] if prev is None else [prev])


def _rmsnorm_kernel(x_ref, w_ref, o_ref, *, eps):
    x = x_ref[...].astype(F32)
    ms = jnp.mean(x * x, axis=-1, keepdims=True)
    o_ref[...] = (x * lax.rsqrt(ms + eps) * w_ref[...]).astype(o_ref.dtype)


def rmsnorm(x, w, eps, out_dtype):
    m, d = x.shape
    tm = _tile(m, 1024)
    return pl.pallas_call(
        functools.partial(_rmsnorm_kernel, eps=eps),
        grid=(m // tm,),
        in_specs=[pl.BlockSpec((tm, d), lambda i: (i, 0)), _const_spec((1, d))],
        out_specs=pl.BlockSpec((tm, d), lambda i: (i, 0)),
        out_shape=jax.ShapeDtypeStruct((m, d), out_dtype),
        compiler_params=_cparams(("parallel",)),
        name="rmsnorm",
    )(x, w.reshape(1, d))


_SEG_START = dict(z=0, xs=1024, bc=2048, dt=3072, cf_a=3088, cf_g=4112, cf_gate=5136, sc_b=6160,
                  sc_c=7184, sc_v=8208, sc_gate=9232, q=10256, k=11280, att_gate=11792, merge=12816)
PROJ_TN = 1024


def _proj_block_starts():
    starts = [_SEG_START["merge"] + PROJ_TN * j for j in range(N_BRANCH * D_MODEL // PROJ_TN)]
    starts += [_SEG_START[name] for name in ("z", "xs", "bc", "cf_a", "cf_g", "cf_gate", "sc_b", "sc_c",
                                              "sc_v", "sc_gate", "q", "att_gate")]
    starts.append(_SEG_START["k"])
    assert len(starts) * PROJ_TN == PROJ_COLS and all(v % SUBLANES == 0 for v in starts)
    return jnp.array([v // SUBLANES for v in starts], jnp.int32)


def _inproj_kernel(starts_ref, a_ref, w_ref, o_ref):
    del starts_ref
    w = w_ref[0].astype(BF16)
    o_ref[...] = lax.dot_general(a_ref[...], w, (((1,), (1,)), ((), ())),
                                 preferred_element_type=F32).astype(o_ref.dtype)


def in_proj(xn, w_t, l, tm):
    m, k = xn.shape
    tm = _tile(m, tm)
    grid_spec = pltpu.PrefetchScalarGridSpec(
        num_scalar_prefetch=1, grid=(m // tm, PROJ_COLS // PROJ_TN),
        in_specs=[pl.BlockSpec((tm, k), lambda i, j, s: (i, 0)),
                  pl.BlockSpec((pl.Element(1), pl.Element(PROJ_TN), pl.Element(k)),
                               lambda i, j, s: (l, s[j] * SUBLANES, 0))],
        out_specs=pl.BlockSpec((tm, PROJ_TN), lambda i, j, s: (i, j)))
    return pl.pallas_call(
        _inproj_kernel, grid_spec=grid_spec,
        out_shape=jax.ShapeDtypeStruct((m, PROJ_COLS), BF16),
        compiler_params=_cparams(("parallel", "arbitrary")),
        name="in_proj",
    )(_proj_block_starts(), xn, w_t)


def _out_kernel(h_ref, w_ref, x_ref, nw_ref, *out_refs, eps, last):
    x = x_ref[...] + jnp.dot(h_ref[...], w_ref[...], preferred_element_type=F32)
    ms = jnp.mean(x * x, axis=-1, keepdims=True)
    xn = x * lax.rsqrt(ms + eps) * nw_ref[...]
    if last:
        out_refs[0][...] = xn
    else:
        out_refs[0][...] = x
        out_refs[1][...] = xn.astype(out_refs[1].dtype)


def out_proj_norm(h, w, x, norm_w, eps, last, tm):
    m, k = h.shape
    n = w.shape[1]
    tm = _tile(m, tm)
    row = lambda i: (i, 0)
    if last:
        out_shape = [jax.ShapeDtypeStruct((m, n), F32)]
    else:
        out_shape = [jax.ShapeDtypeStruct((m, n), F32), jax.ShapeDtypeStruct((m, n), BF16)]
    return pl.pallas_call(
        functools.partial(_out_kernel, eps=eps, last=last),
        grid=(m // tm,),
        in_specs=[pl.BlockSpec((tm, k), row),
                  pl.BlockSpec((k, n), lambda i: (0, 0), pipeline_mode=pl.Buffered(1)),
                  pl.BlockSpec((tm, n), row), _const_spec((1, n))],
        out_specs=[pl.BlockSpec((tm, n), row)] * len(out_shape),
        out_shape=out_shape,
        compiler_params=_cparams(("parallel",)),
        name="out_proj_norm",
    )(h, w, x, norm_w.reshape(1, n))


def _merge_kernel(*refs):
    b_refs, w_refs, m_refs, o_ref = refs[0:4], refs[4:8], refs[8:12], refs[12]
    acc = None
    for b_ref, w_ref, m_ref in zip(b_refs, w_refs, m_refs):
        out = jnp.dot(b_ref[...], w_ref[0], preferred_element_type=F32)
        term = _sigmoid(m_ref[...].astype(F32)) * out
        acc = term if acc is None else acc + term
    o_ref[...] = acc.astype(o_ref.dtype)


def merge_branches(branches, w_outs, proj, l, tm, tn):
    m = proj.shape[0]
    tm = _tile(m, tm)
    nb = D_MODEL // tn
    br_spec = pl.BlockSpec((tm, WIDTH), lambda i, j: (i, 0))
    w_spec = pl.BlockSpec((1, WIDTH, tn), lambda i, j: (l, 0, j))
    merge_specs = [pl.BlockSpec((tm, tn), functools.partial(lambda i, j, b: (i, b * nb + j), b=b))
                   for b in range(N_BRANCH)]
    return pl.pallas_call(
        _merge_kernel,
        grid=(m // tm, nb),
        in_specs=[br_spec] * N_BRANCH + [w_spec] * N_BRANCH + merge_specs,
        out_specs=pl.BlockSpec((tm, tn), lambda i, j: (i, j)),
        out_shape=jax.ShapeDtypeStruct((m, D_MODEL), BF16),
        compiler_params=_cparams(("parallel", "arbitrary")),
        name="merge_branches",
    )(*branches, *w_outs, proj, proj, proj, proj)


def _pad_rows(x, rows):
    if x.shape[0] == rows:
        return x
    return jnp.concatenate([x, jnp.zeros((rows - x.shape[0], x.shape[1]), x.dtype)], axis=0)


def _bf16_terms(x, n):
    terms = []
    for _ in range(n):
        t = x.astype(BF16)
        terms.append(t)
        x = x - t.astype(F32)
    return terms


def _head_stats(dtr_all, dtb, alog, tri3, e2, *, L, nb):
    dtr_all = dtr_all + dtb
    dt_all = jnp.maximum(dtr_all, 0.0) + jnp.log1p(jnp.exp(-jnp.abs(dtr_all)))
    da_all = dt_all * (-jnp.exp(alog))
    css, ecss, decs = [], [], []
    for bi in range(nb):
        da = da_all[bi * L:(bi + 1) * L, :]
        if L == SSM_CHUNK:
            cs = jnp.dot(tri3, jnp.concatenate(_bf16_terms(da, 3), axis=0), preferred_element_type=F32)
        else:
            row = lax.broadcasted_iota(jnp.int32, (L, LANES), 0)
            cs = jnp.zeros((L, LANES), F32)
            for s in range(L):
                cs = cs + jnp.where(row >= s, da[s:s + 1, :], 0.0)
        css.append(cs)
        ecss.append(jnp.exp(cs))
        decs.append(jnp.exp(cs[L - 1:L, :] - cs))
    stack = jnp.concatenate([dt_all] + ecss + decs, axis=0)
    ex = jnp.dot(jnp.concatenate(_bf16_terms(stack, 2), axis=1), e2, preferred_element_type=F32)
    n = nb * L
    return [(css[bi], ex[bi * L:(bi + 1) * L], ex[n + bi * L:n + (bi + 1) * L],
             ex[2 * n + bi * L:2 * n + (bi + 1) * L]) for bi in range(nb)]


def _ssd_core(xbc, z, stats, h_ref, dsk, nw, *, L):
    cs, dt_e, ecs_e, dec_e = stats
    mm = BF16 if L >= 16 else F32
    xs = xbc[:, 0:WIDTH]
    bmat = xbc[:, WIDTH:WIDTH + SSM_GROUPS * SSM_STATE]
    cmat = xbc[:, WIDTH + SSM_GROUPS * SSM_STATE:]
    xdt = xs * dt_e
    xdd = xdt * dec_e

    lp = SSM_CHUNK
    xdt_p = _pad_rows(xdt, lp).astype(mm)
    xdd_p = _pad_rows(xdd, lp)
    b_p = _pad_rows(bmat, lp).astype(mm)
    cs_t = _pad_rows(cs, lp).T
    cmat_m = cmat.astype(mm)

    li = lax.broadcasted_iota(jnp.int32, (L, lp), 0)
    sj = lax.broadcasted_iota(jnp.int32, (L, lp), 1)
    causal = sj <= li
    lane = lax.broadcasted_iota(jnp.int32, (lp, LANES), 1)
    lo_half = lane < SSM_HEADDIM

    y_blocks = []
    hpg = SSM_HEADS // SSM_GROUPS
    gw = hpg * SSM_HEADDIM
    bpg = gw // LANES
    for g in range(SSM_GROUPS):
        c_g = cmat_m[:, g * SSM_STATE:(g + 1) * SSM_STATE]
        b_g = b_p[:, g * SSM_STATE:(g + 1) * SSM_STATE]
        cb = lax.dot_general(c_g, b_g, (((1,), (1,)), ((), ())), preferred_element_type=F32)
        h_blk = h_ref[g * gw:(g + 1) * gw, :]
        y_off = lax.dot_general(c_g, h_blk.astype(mm), (((1,), (1,)), ((), ())),
                                preferred_element_type=F32)
        y_off = y_off * ecs_e[:, g * gw:(g + 1) * gw]
        for jb in range(bpg):
            blk = g * bpg + jb
            x_blk = xdt_p[:, blk * LANES:(blk + 1) * LANES]
            acc = None
            for par in range(2):
                h = 2 * blk + par
                diff = cs[:, h:h + 1] - cs_t[h:h + 1, :]
                lm = jnp.exp(jnp.where(causal, diff, -jnp.inf))
                gmat = (cb * lm).astype(mm)
                x_half = jnp.where(lo_half if par == 0 else jnp.logical_not(lo_half), x_blk,
                                   jnp.zeros_like(x_blk))
                t = jnp.dot(gmat, x_half, preferred_element_type=F32)
                acc = t if acc is None else acc + t
            y_blocks.append(acc + y_off[:, jb * LANES:(jb + 1) * LANES])
        xdd_t = jnp.concatenate(
            [xdd_p[:, (g * bpg + jb) * LANES:(g * bpg + jb + 1) * LANES].T for jb in range(bpg)],
            axis=0).astype(mm)
        s_new = jnp.dot(xdd_t, b_g, preferred_element_type=F32)
        for r in range(hpg):
            h = g * hpg + r
            scal = jnp.exp(cs_t[h:h + 1, L - 1:L])
            rows = slice(h * SSM_HEADDIM, (h + 1) * SSM_HEADDIM)
            h_ref[rows, :] = h_ref[rows, :] * scal + s_new[r * SSM_HEADDIM:(r + 1) * SSM_HEADDIM, :]

    y = jnp.concatenate(y_blocks, axis=1) + dsk * xs
    y = y * _silu(z)
    ms = jnp.mean(y * y, axis=-1, keepdims=True)
    return y * lax.rsqrt(ms + 1e-5) * nw


def _ssd_kernel(*refs, L, nc, nb, zero_state, n_alias):
    xs_ref, bc_ref, z_ref, xn_ref, wdt_ref = refs[:5]
    refs = refs[5:]
    if not zero_state:
        pre_ref, h0_ref = refs[:2]
        refs = refs[2:]
    cw_ref, cb_ref, dtb_ref, alog_ref, dsk_ref, nw_ref, tri_ref, e_ref = refs[:8]
    y_ref, hout_ref, cst_ref, xp_ref, h_ref = refs[8 + n_alias:]
    c = pl.program_id(1)
    off = SUBLANES - (SSM_CONV - 1)

    dtr_all = lax.dot_general(xn_ref[...], wdt_ref[0].astype(BF16), (((1,), (1,)), ((), ())),
                              preferred_element_type=F32)
    stats = _head_stats(dtr_all, dtb_ref[...], alog_ref[...], tri_ref[...], e_ref[...], L=L, nb=nb)
    xs_all = xs_ref[...].astype(F32)
    bc_all = bc_ref[...].astype(F32)
    z_all = z_ref[...].astype(F32)
    ys = []
    for bi in range(nb):
        rows = slice(bi * L, (bi + 1) * L)

        @pl.when(c == 0)
        def _():
            if zero_state:
                xp_ref[bi, off:SUBLANES, :] = jnp.zeros((SSM_CONV - 1, 2 * WIDTH), F32)
                h_ref[bi] = jnp.zeros((WIDTH, SSM_STATE), F32)
            else:
                xp_ref[bi, off:SUBLANES, :] = pre_ref[0, bi]
                h_ref[bi] = h0_ref[0, bi]

        if nc > 1:
            @pl.when(c > 0)
            def _():
                xp_ref[bi, off:SUBLANES, :] = xp_ref[bi, L + off:L + SUBLANES, :]

        xp_ref[bi, SUBLANES:SUBLANES + L, 0:WIDTH] = xs_all[rows, :]
        xp_ref[bi, SUBLANES:SUBLANES + L, WIDTH:2 * WIDTH] = bc_all[rows, :]

        conv = cb_ref[...] + cw_ref[0:1, :] * xp_ref[bi, off:off + L, :]
        for k in range(1, SSM_CONV):
            conv = conv + cw_ref[k:k + 1, :] * xp_ref[bi, off + k:off + k + L, :]
        ys.append(_ssd_core(_silu(conv), z_all[rows, :], stats[bi], h_ref.at[bi], dsk_ref[...],
                            nw_ref[...], L=L))

        @pl.when(c == nc - 1)
        def _():
            hout_ref[0, bi] = h_ref[bi]
            cst_ref[0, bi] = xp_ref[bi, L + off:L + SUBLANES, :]

    y_ref[...] = jnp.concatenate(ys, axis=0).astype(y_ref.dtype)


def ssd_branch(proj, xn, w_t, lp, consts, l, *, nbatch, t, conv_state=None, ssm_state=None, prev=None):
    L = SSM_CHUNK if t % SSM_CHUNK == 0 else t
    nc = t // L
    zero_state = conv_state is None
    nb = 8 if nc == 1 and nbatch % 8 == 0 else 1
    m = nbatch * t
    rl = nb * L

    def rows(b, c):
        return b * nc + c

    def col(cidx):
        return pl.BlockSpec((rl, WIDTH), lambda b, c: (rows(b, c), cidx))

    tri, e = consts
    in_specs = [col(COL_XS), col(COL_BC), col(COL_Z),
                pl.BlockSpec((rl, D_MODEL), lambda b, c: (rows(b, c), 0)),
                pl.BlockSpec((1, LANES, D_MODEL), lambda b, c: (l, _SEG_START["dt"] // LANES, 0))]
    args = [proj, proj, proj, xn, w_t]
    if not zero_state:
        in_specs += [pl.BlockSpec((1, nb, SSM_CONV - 1, 2 * WIDTH), lambda b, c: (l, b, 0, 0)),
                     pl.BlockSpec((1, nb, WIDTH, SSM_STATE), lambda b, c: (l, b, 0, 0))]
        args += [conv_state, ssm_state]
    in_specs += [_const_spec((SSM_CONV, 2 * WIDTH)), _const_spec((1, 2 * WIDTH)),
                 _const_spec((1, LANES)), _const_spec((1, LANES)), _const_spec((1, WIDTH)),
                 _const_spec((1, WIDTH)), _const_spec((SSM_CHUNK, 3 * SSM_CHUNK)),
                 _const_spec((2 * LANES, WIDTH))]
    args += [lp["ssm_conv_w"], lp["ssm_conv_b"], lp["dt_bias"], lp["a_log"], lp["d_skip"],
             lp["ssm_norm_w"], tri, e]
    h_shape, h_spec, h_prev = _stacked_out(l, nbatch, nb, (WIDTH, SSM_STATE), prev and prev[0])
    c_shape, c_spec, c_prev = _stacked_out(l, nbatch, nb, (SSM_CONV - 1, 2 * WIDTH), prev and prev[1])
    alias_in = h_prev + c_prev
    aliases = {len(args) + i: 1 + i for i in range(len(alias_in))}
    in_specs += [pl.BlockSpec(memory_space=pl.ANY)] * len(alias_in)
    y, h_out, c_out = pl.pallas_call(
        functools.partial(_ssd_kernel, L=L, nc=nc, nb=nb, zero_state=zero_state, n_alias=len(alias_in)),
        grid=(nbatch // nb, nc),
        in_specs=in_specs,
        out_specs=[pl.BlockSpec((rl, WIDTH), lambda b, c: (rows(b, c), 0)), h_spec, c_spec],
        out_shape=[jax.ShapeDtypeStruct((m, WIDTH), BF16), h_shape, c_shape],
        scratch_shapes=[pltpu.VMEM((nb, SUBLANES + L, 2 * WIDTH), F32),
                        pltpu.VMEM((nb, WIDTH, SSM_STATE), F32)],
        input_output_aliases=aliases,
        compiler_params=_cparams(("parallel", "arbitrary")),
        name="ssd_branch",
    )(*args, *alias_in)
    return y, (h_out, c_out)


def _dwconv(xp_ref, xsh_ref, w_ref, b_ref, acc_ref, *, k, start, tb, row0):
    rbk = min(tb, 32)
    if xsh_ref is not None:
        nrows = xsh_ref.shape[1]
        for s in range(1, SUBLANES):
            xsh_ref[s - 1] = xp_ref[s:s + nrows, :]
    for j in range(WIDTH // LANES):
        lanes = slice(j * LANES, (j + 1) * LANES)
        for r in range(tb // rbk):
            r0 = r * rbk
            acc = None
            for i in range(k):
                first = start + i + r0
                s = first % SUBLANES
                if xsh_ref is None or s == 0:
                    x = xp_ref[first:first + rbk, lanes]
                else:
                    x = xsh_ref[s - 1, first - s:first - s + rbk, lanes]
                term = w_ref[i:i + 1, lanes] * x
                if acc is None:
                    acc = term if b_ref is None else term + b_ref[:, lanes]
                else:
                    acc = acc + term
            acc_ref[row0 + r0:row0 + r0 + rbk, lanes] = acc


def _conv_all(u, xp_ref, pre_ref, st_ref, w_ref, b_ref, acc_ref, xsh_ref=None, *, k, p, tb, nc, nb):
    c = pl.program_id(1)
    for bi in range(nb):
        @pl.when(c == 0)
        def _():
            if pre_ref is None:
                xp_ref[bi, p - (k - 1):p, :] = jnp.zeros((k - 1, WIDTH), F32)
            else:
                xp_ref[bi, p - (k - 1):p, :] = pre_ref[0, bi]

        if nc > 1:
            @pl.when(c > 0)
            def _():
                xp_ref[bi, 0:p, :] = xp_ref[bi, tb:tb + p, :]

        xp_ref[bi, p:p + tb, :] = u[bi * tb:(bi + 1) * tb, :]
        _dwconv(xp_ref.at[bi], xsh_ref, w_ref, b_ref, acc_ref, k=k, start=p - (k - 1), tb=tb,
                row0=bi * tb)

        @pl.when(c == nc - 1)
        def _():
            st_ref[0, bi] = xp_ref[bi, p + tb - (k - 1):p + tb, :]


def _cf_kernel(*refs, tb, nc, nb, p, zero_state, n_alias):
    a_ref, g_ref, gate_ref = refs[:3]
    refs = refs[3:]
    pre_ref = None
    if not zero_state:
        pre_ref, refs = refs[0], refs[1:]
    w_ref, b_ref, lnw_ref, lnb_ref = refs[:4]
    o_ref, st_ref, xp_ref, acc_ref, *xsh = refs[4 + n_alias:]
    u = a_ref[...].astype(F32) * _sigmoid(g_ref[...].astype(F32))
    _conv_all(u, xp_ref, pre_ref, st_ref, w_ref, b_ref, acc_ref, *xsh, k=CF_CONV, p=p, tb=tb, nc=nc,
              nb=nb)
    u = acc_ref[...]
    mu = jnp.mean(u, axis=-1, keepdims=True)
    uc = u - mu
    var = jnp.mean(uc * uc, axis=-1, keepdims=True)
    u = _silu(uc * lax.rsqrt(var + 1e-5) * lnw_ref[...] + lnb_ref[...])
    o_ref[...] = (u * _silu(gate_ref[...].astype(F32))).astype(o_ref.dtype)


def _sc_kernel(*refs, tb, nc, nb, p, zero_state, n_alias):
    gb_ref, gc_ref, sv_ref, gate_ref = refs[:4]
    refs = refs[4:]
    pre_ref = None
    if not zero_state:
        pre_ref, refs = refs[0], refs[1:]
    w_ref = refs[0]
    o_ref, st_ref, xp_ref, acc_ref = refs[1 + n_alias:]
    u = gc_ref[...].astype(F32) * sv_ref[...].astype(F32)
    _conv_all(u, xp_ref, pre_ref, st_ref, w_ref, None, acc_ref, k=SC_CONV, p=p, tb=tb, nc=nc, nb=nb)
    o_ref[...] = (gb_ref[...].astype(F32) * acc_ref[...]
                  * _silu(gate_ref[...].astype(F32))).astype(o_ref.dtype)


def conv_branch(kind, proj, lp, l, *, nbatch, t, state=None, prev=None):
    tb = min(t, 256)
    nc = t // tb
    nb = 8 if nc == 1 and nbatch % 8 == 0 else 1
    m = nbatch * t
    rl = nb * tb
    zero_state = state is None
    k = CF_CONV if kind == "cf" else SC_CONV
    p = -(-(k - 1) // SUBLANES) * SUBLANES

    def rows(b, c):
        return b * nc + c

    def col(cidx):
        return pl.BlockSpec((rl, WIDTH), lambda b, c: (rows(b, c), cidx))

    if kind == "cf":
        in_specs = [col(COL_CF_A), col(COL_CF_G), col(COL_CF_GATE)]
        args = [proj, proj, proj]
        w_specs = [_const_spec((k, WIDTH)), _const_spec((1, WIDTH)), _const_spec((1, WIDTH)),
                   _const_spec((1, WIDTH))]
        w_args = [lp["cf_conv_w"], lp["cf_conv_b"], lp["cf_ln_w"], lp["cf_ln_b"]]
        body = _cf_kernel
    else:
        in_specs = [col(COL_SC_B), col(COL_SC_C), col(COL_SC_V), col(COL_SC_GATE)]
        args = [proj, proj, proj, proj]
        w_specs = [_const_spec((k, WIDTH))]
        w_args = [lp["sc_conv_w"]]
        body = _sc_kernel
    if not zero_state:
        in_specs.append(pl.BlockSpec((1, nb, k - 1, WIDTH), lambda b, c: (l, b, 0, 0)))
        args.append(state)
    in_specs += w_specs
    args += w_args
    s_shape, s_spec, alias_in = _stacked_out(l, nbatch, nb, (k - 1, WIDTH), prev)
    aliases = {len(args) + i: 1 + i for i in range(len(alias_in))}
    in_specs += [pl.BlockSpec(memory_space=pl.ANY)] * len(alias_in)
    scratch = [pltpu.VMEM((nb, p + tb, WIDTH), F32), pltpu.VMEM((rl, WIDTH), F32)]
    if kind == "cf" and nb == 1 and tb % 32 == 0:
        scratch.append(pltpu.VMEM((SUBLANES - 1, p + tb - SUBLANES, WIDTH), F32))
    return pl.pallas_call(
        functools.partial(body, tb=tb, nc=nc, nb=nb, p=p, zero_state=zero_state, n_alias=len(alias_in)),
        grid=(nbatch // nb, nc),
        in_specs=in_specs,
        out_specs=[pl.BlockSpec((rl, WIDTH), lambda b, c: (rows(b, c), 0)), s_spec],
        out_shape=[jax.ShapeDtypeStruct((m, WIDTH), BF16), s_shape],
        scratch_shapes=scratch,
        input_output_aliases=aliases,
        compiler_params=_cparams(("parallel", "arbitrary")),
        name=kind + "_branch",
    )(*args, *alias_in)


def _sink_column(sink_ref, g, rows_per_head):
    n = ATT_REP * rows_per_head
    rid = lax.broadcasted_iota(jnp.int32, (n, 1), 0)
    col = jnp.full((n, 1), sink_ref[g * ATT_REP + ATT_REP - 1], F32)
    for r in range(ATT_REP - 2, -1, -1):
        col = jnp.where(rid < (r + 1) * rows_per_head, sink_ref[g * ATT_REP + r], col)
    return col


def _softmax_sink(s, ok, sink_col):
    s = jnp.where(ok, s, -jnp.inf)
    m = jnp.maximum(jnp.max(s, axis=-1, keepdims=True), sink_col)
    p = jnp.exp(s - m)
    denom = jnp.sum(p, axis=-1, keepdims=True) + jnp.exp(sink_col - m)
    return p / denom


def _attn_prompt_kernel(q_ref, kp_ref, kc_ref, vp_ref, vc_ref, gate_ref, sink_ref, *rest, nblk, n_alias):
    o_ref, kw_ref, vw_ref = rest[n_alias:]
    L = WINDOW
    i = pl.program_id(1)
    q = q_ref[...].astype(F32) * (ATT_HEAD_DIM ** -0.5)
    k2 = jnp.concatenate([kp_ref[...], kc_ref[...]], axis=0).astype(F32)
    v2 = jnp.concatenate([vp_ref[...], vc_ref[...]], axis=0).astype(F32)
    li = lax.broadcasted_iota(jnp.int32, (ATT_REP * L, 2 * L), 0) & (L - 1)
    sj = lax.broadcasted_iota(jnp.int32, (ATT_REP * L, 2 * L), 1)
    ok = (sj > li) & (sj <= li + WINDOW) & ((sj >= L) | (i > 0))
    lane = lax.broadcasted_iota(jnp.int32, (2 * L, LANES), 1)
    lo_half = lane < ATT_HEAD_DIM

    o_blocks = [None] * (WIDTH // LANES)
    for g in range(ATT_KV_HEADS):
        half = g % 2
        sl = slice((g // 2) * LANES, (g // 2 + 1) * LANES)
        keep = lo_half if half == 0 else jnp.logical_not(lo_half)
        k_use = jnp.where(keep, k2[:, sl], 0.0).astype(BF16)
        v_use = jnp.where(keep, v2[:, sl], 0.0).astype(BF16)
        parts = []
        for r in range(ATT_REP):
            h = g * ATT_REP + r
            qb = q[:, (h // 2) * LANES:(h // 2 + 1) * LANES]
            parts.append(qb if h % 2 == half else pltpu.roll(qb, ATT_HEAD_DIM, 1))
        lhs = jnp.concatenate(parts, axis=0).astype(BF16)
        s = lax.dot_general(lhs, k_use, (((1,), (1,)), ((), ())), preferred_element_type=F32)
        pn = _softmax_sink(s, ok, _sink_column(sink_ref, g, L)).astype(BF16)
        o = jnp.dot(pn, v_use, preferred_element_type=F32)
        for r in range(ATT_REP):
            h = g * ATT_REP + r
            ob = o[r * L:(r + 1) * L, :]
            if h % 2 != half:
                ob = pltpu.roll(ob, ATT_HEAD_DIM, 1)
            o_blocks[h // 2] = ob if o_blocks[h // 2] is None else o_blocks[h // 2] + ob
    o = jnp.concatenate(o_blocks, axis=1)
    o_ref[...] = (o * _silu(gate_ref[...].astype(F32))).astype(o_ref.dtype)

    @pl.when(i == nblk - 1)
    def _():
        kw_ref[0, 0] = kc_ref[...].astype(F32).T.reshape(ATT_KV_HEADS, ATT_HEAD_DIM, WINDOW)
        vw_ref[0, 0] = vc_ref[...].astype(F32).T.reshape(ATT_KV_HEADS, ATT_HEAD_DIM, WINDOW)


def _win_out(l, nbatch, nb, prev):
    return _stacked_out(l, nbatch, nb, (ATT_KV_HEADS, ATT_HEAD_DIM, WINDOW), prev)


def attn_prompt(proj, sinks, l, *, nbatch, t, prev=None):
    L = WINDOW
    nblk = t // L

    def col(cidx):
        return pl.BlockSpec((L, WIDTH), lambda b, i: (b * nblk + i, cidx))

    def kv(cidx, back):
        return pl.BlockSpec((L, ATT_KV_WIDTH), lambda b, i: (b * nblk + jnp.maximum(i - back, 0), cidx))

    k_shape, k_spec, k_prev = _win_out(l, nbatch, 1, prev and prev[0])
    v_shape, v_spec, v_prev = _win_out(l, nbatch, 1, prev and prev[1])
    alias_in = k_prev + v_prev
    args = [proj, proj, proj, proj, proj, proj, sinks]
    aliases = {len(args) + i: 1 + i for i in range(len(alias_in))}
    y, kw, vw = pl.pallas_call(
        functools.partial(_attn_prompt_kernel, nblk=nblk, n_alias=len(alias_in)),
        grid=(nbatch, nblk),
        in_specs=[col(COL_Q), kv(COL_K, 1), kv(COL_K, 0), kv(COL_V, 1), kv(COL_V, 0),
                  col(COL_ATT_GATE), pl.BlockSpec(memory_space=pltpu.SMEM)]
        + [pl.BlockSpec(memory_space=pl.ANY)] * len(alias_in),
        out_specs=[pl.BlockSpec((L, WIDTH), lambda b, i: (b * nblk + i, 0)), k_spec, v_spec],
        out_shape=[jax.ShapeDtypeStruct((nbatch * t, WIDTH), BF16), k_shape, v_shape],
        input_output_aliases=aliases,
        compiler_params=_cparams(("parallel", "arbitrary")),
        name="attn_prompt",
    )(*args, *alias_in)
    return y, (kw, vw)


def _attn_sample_kernel(q_ref, kn_ref, vn_ref, ck_ref, cv_ref, gate_ref, sink_ref, *rest, L, nb, n_alias):
    o_ref, kw_ref, vw_ref = rest[n_alias:]
    q = q_ref[...].astype(F32) * (ATT_HEAD_DIM ** -0.5)
    kn_t = _pad_rows(kn_ref[...].astype(F32), WINDOW).T
    vn_t = _pad_rows(vn_ref[...].astype(F32), WINDOW).T
    nkeys = 2 * WINDOW
    rq = ATT_REP * L
    row = lax.broadcasted_iota(jnp.int32, (nb * rq, nkeys), 0)
    sj = lax.broadcasted_iota(jnp.int32, (nb * rq, nkeys), 1)
    li = row & (L - 1)
    cj = sj - WINDOW - (row // rq) * L
    ok = ((sj < WINDOW) & (sj > li)) | ((cj >= 0) & (cj <= li))
    head_of_row = (lax.broadcasted_iota(jnp.int32, (nb * rq, 1), 0) // L) % ATT_REP
    lane = lax.broadcasted_iota(jnp.int32, (ATT_HEAD_DIM, WINDOW), 1)
    is_new = lane >= WINDOW - L
    pieces = [[None] * ATT_HEADS for _ in range(nb)]
    for g in range(ATT_KV_HEADS):
        gl = slice(g * ATT_HEAD_DIM, (g + 1) * ATT_HEAD_DIM)
        kn_g, vn_g = kn_t[gl, :], vn_t[gl, :]
        scores, values = [], []
        for bi in range(nb):
            rows = slice(bi * L, (bi + 1) * L)
            kc, vc = ck_ref[0, bi, g], cv_ref[0, bi, g]
            k2 = jnp.concatenate([kc, kn_g], axis=1).astype(BF16)
            values.append(jnp.concatenate([vc, vn_g], axis=1).astype(BF16))
            lhs = jnp.concatenate(
                [q[rows, (g * ATT_REP + r) * ATT_HEAD_DIM:(g * ATT_REP + r + 1) * ATT_HEAD_DIM]
                 for r in range(ATT_REP)], axis=0).astype(BF16)
            scores.append(jnp.dot(lhs, k2, preferred_element_type=F32))
            shift_new = (WINDOW - L - bi * L) % WINDOW
            for c_old, n_g, w_ref in ((kc, kn_g, kw_ref), (vc, vn_g, vw_ref)):
                tail = n_g if shift_new == 0 else pltpu.roll(n_g, shift_new, 1)
                w_ref[0, bi, g] = jnp.where(is_new, tail, pltpu.roll(c_old, WINDOW - L, 1))
        sink_col = jnp.full((nb * rq, 1), sink_ref[g * ATT_REP], F32)
        for r in range(1, ATT_REP):
            sink_col = jnp.where(head_of_row == r, sink_ref[g * ATT_REP + r], sink_col)
        pn = _softmax_sink(jnp.concatenate(scores, axis=0), ok, sink_col).astype(BF16)
        for bi in range(nb):
            o = lax.dot_general(pn[bi * rq:(bi + 1) * rq, :], values[bi], (((1,), (1,)), ((), ())),
                                preferred_element_type=F32)
            for r in range(ATT_REP):
                pieces[bi][g * ATT_REP + r] = o[r * L:(r + 1) * L, :]
    o = jnp.concatenate([jnp.concatenate(p, axis=1) for p in pieces], axis=0)
    o_ref[...] = (o * _silu(gate_ref[...].astype(F32))).astype(o_ref.dtype)


def attn_sample(proj, sinks, cache_k, cache_v, l, *, nbatch, t, prev=None):
    nb = WINDOW // t if nbatch % (WINDOW // t) == 0 else nbatch
    rl = nb * t

    def col(cidx, w):
        return pl.BlockSpec((rl, w), lambda b: (b, cidx))

    cache = pl.BlockSpec((1, nb, ATT_KV_HEADS, ATT_HEAD_DIM, WINDOW), lambda b: (l, b, 0, 0, 0))
    k_shape, k_spec, k_prev = _win_out(l, nbatch, nb, prev and prev[0])
    v_shape, v_spec, v_prev = _win_out(l, nbatch, nb, prev and prev[1])
    alias_in = k_prev + v_prev
    args = [proj, proj, proj, cache_k, cache_v, proj, sinks]
    aliases = {len(args) + i: 1 + i for i in range(len(alias_in))}
    y, kw, vw = pl.pallas_call(
        functools.partial(_attn_sample_kernel, L=t, nb=nb, n_alias=len(alias_in)),
        grid=(nbatch // nb,),
        in_specs=[col(COL_Q, WIDTH), col(COL_K, ATT_KV_WIDTH), col(COL_V, ATT_KV_WIDTH), cache, cache,
                  col(COL_ATT_GATE, WIDTH), pl.BlockSpec(memory_space=pltpu.SMEM)]
        + [pl.BlockSpec(memory_space=pl.ANY)] * len(alias_in),
        out_specs=[pl.BlockSpec((rl, WIDTH), lambda b: (b, 0)), k_spec, v_spec],
        out_shape=[jax.ShapeDtypeStruct((nbatch * t, WIDTH), BF16), k_shape, v_shape],
        input_output_aliases=aliases,
        compiler_params=_cparams(("parallel",)),
        name="attn_sample",
    )(*args, *alias_in)
    return y, (kw, vw)


def _prep_layer(l, norm_w, w_in, ssm_conv_w, ssm_conv_b, ssm_dt_bias, ssm_a_log, ssm_d, ssm_norm_w,
                w_out_ssm, cf_conv_w, cf_conv_b, cf_ln_w, cf_ln_b, w_out_cf, sc_conv_w, w_out_sc,
                att_sinks, w_out_att, w_o):
    del w_in
    pad_h = (0, LANES - SSM_HEADS)
    return dict(
        norm_w=norm_w[l],
        ssm_conv_w=ssm_conv_w[l], ssm_conv_b=ssm_conv_b[l].reshape(1, -1),
        dt_bias=jnp.pad(ssm_dt_bias[l], pad_h).reshape(1, LANES),
        a_log=jnp.pad(ssm_a_log[l], pad_h).reshape(1, LANES),
        d_skip=jnp.repeat(ssm_d[l], SSM_HEADDIM).reshape(1, WIDTH),
        ssm_norm_w=ssm_norm_w[l].reshape(1, WIDTH),
        cf_conv_w=cf_conv_w[l], cf_conv_b=cf_conv_b[l].reshape(1, WIDTH),
        cf_ln_w=cf_ln_w[l].reshape(1, WIDTH), cf_ln_b=cf_ln_b[l].reshape(1, WIDTH),
        sc_conv_w=sc_conv_w[l], sinks=att_sinks[l],
        w_outs=tuple(w.astype(BF16) for w in (w_out_ssm, w_out_cf, w_out_sc, w_out_att)),
        w_o=w_o[l].astype(BF16),
    )


def _layer(x, xn, w_t, lp, consts, l, states, prev, next_norm_w, *, nbatch, t):
    kw = dict(nbatch=nbatch, t=t)
    proj = in_proj(xn, w_t, l, tm=2048)
    if states is None:
        ya, st_a = ssd_branch(proj, xn, w_t, lp, consts, l, prev=prev and prev[0], **kw)
        yb, st_b = conv_branch("cf", proj, lp, l, prev=prev and prev[1], **kw)
        yc, st_c = conv_branch("sc", proj, lp, l, prev=prev and prev[2], **kw)
        yd, st_d = attn_prompt(proj, lp["sinks"], l, prev=prev and prev[3], **kw)
    else:
        ssm, conv_ssm, conv_cf, conv_sc, cache_k, cache_v = states
        ya, st_a = ssd_branch(proj, xn, w_t, lp, consts, l, conv_state=conv_ssm, ssm_state=ssm,
                              prev=prev and prev[0], **kw)
        yb, st_b = conv_branch("cf", proj, lp, l, state=conv_cf, prev=prev and prev[1], **kw)
        yc, st_c = conv_branch("sc", proj, lp, l, state=conv_sc, prev=prev and prev[2], **kw)
        yd, st_d = attn_sample(proj, lp["sinks"], cache_k, cache_v, l, prev=prev and prev[3], **kw)
    hmix = merge_branches((ya, yb, yc, yd), lp["w_outs"], proj, l, tm=1024, tn=512)
    outs = out_proj_norm(hmix, lp["w_o"], x, next_norm_w, EPS, last=l == DEPTH - 1, tm=512)
    return outs, (st_a, st_b, st_c, st_d)


def kernel(x_prompt, x_sample, state_ssm, state_conv_ssm, state_conv_cf, state_conv_sc, cache_k, cache_v, norm_w, w_in, ssm_conv_w, ssm_conv_b, ssm_dt_bias, ssm_a_log, ssm_d, ssm_norm_w, w_out_ssm, cf_conv_w, cf_conv_b, cf_ln_w, cf_ln_b, w_out_cf, sc_conv_w, w_out_sc, att_sinks, w_out_att, w_o, final_norm_w):
    bp, tp, _ = x_prompt.shape
    bs, ts, _ = x_sample.shape
    xp = x_prompt.reshape(bp * tp, D_MODEL)
    xs = x_sample.reshape(bs * ts, D_MODEL)

    tri = jnp.tril(jnp.ones((SSM_CHUNK, SSM_CHUNK), BF16))
    expand = (jnp.arange(WIDTH)[None, :] // SSM_HEADDIM == jnp.arange(LANES)[:, None]).astype(BF16)
    consts = (jnp.tile(tri, (1, 3)), jnp.tile(expand, (2, 1)))
    to_win_minor = lambda c: jnp.transpose(c, (0, 1, 3, 4, 2))
    from_win_minor = lambda c: jnp.transpose(c, (0, 1, 4, 2, 3))
    w_t = jnp.swapaxes(w_in, 1, 2)
    s_states = (state_ssm.reshape(DEPTH, bs, WIDTH, SSM_STATE), state_conv_ssm, state_conv_cf,
                state_conv_sc, to_win_minor(cache_k), to_win_minor(cache_v))

    xnp = rmsnorm(xp, norm_w[0], EPS, BF16)
    xns = rmsnorm(xs, norm_w[0], EPS, BF16)
    p_prev = s_prev = None
    for l in range(DEPTH):
        lp = _prep_layer(l, norm_w, w_in, ssm_conv_w, ssm_conv_b, ssm_dt_bias, ssm_a_log, ssm_d,
                         ssm_norm_w, w_out_ssm, cf_conv_w, cf_conv_b, cf_ln_w, cf_ln_b, w_out_cf,
                         sc_conv_w, w_out_sc, att_sinks, w_out_att, w_o)
        next_w = norm_w[l + 1] if l + 1 < DEPTH else final_norm_w
        outp, p_prev = _layer(xp, xnp, w_t, lp, consts, l, None, p_prev, next_w, nbatch=bp, t=tp)
        outs, s_prev = _layer(xs, xns, w_t, lp, consts, l, s_states, s_prev, next_w, nbatch=bs, t=ts)
        if l + 1 < DEPTH:
            (xp, xnp), (xs, xns) = outp, outs
    y_prompt = outp[0].reshape(bp, tp, D_MODEL)
    y_sample = outs[0].reshape(bs, ts, D_MODEL)

    def unpack(st, nbatch):
        (h, conv_ssm), conv_cf, conv_sc, (kwin, vwin) = st
        return (h.reshape(DEPTH, nbatch, SSM_HEADS, SSM_HEADDIM, SSM_STATE), conv_ssm, conv_cf,
                conv_sc, from_win_minor(kwin), from_win_minor(vwin))

    return (y_prompt, y_sample, *unpack(p_prev, bp), *unpack(s_prev, bs))
```

```python
import functools

import jax
import jax.numpy as jnp
from jax import lax
from jax.experimental import pallas as pl
from jax.experimental.pallas import tpu as pltpu

F32 = jnp.float32
BF16 = jnp.bfloat16

D_MODEL = 2048
DEPTH = 2
N_BRANCH = 4
WIDTH = 1024
SSM_HEADS = 16
SSM_HEADDIM = 64
SSM_GROUPS = 4
SSM_STATE = 128
SSM_CONV = 4
SSM_CHUNK = 128
CF_CONV = 31
SC_CONV = 3
ATT_HEADS = 16
ATT_KV_HEADS = 4
ATT_HEAD_DIM = 64
ATT_KV_WIDTH = ATT_KV_HEADS * ATT_HEAD_DIM
ATT_REP = ATT_HEADS // ATT_KV_HEADS
WINDOW = 128
EPS = 1e-6

LANES = 128
SUBLANES = 8
VMEM_LIMIT = 56 * 1024 * 1024

COL_MERGE = 0
COL_Z = 8
COL_XS = 9
COL_BC = 10
COL_CF_A = 11
COL_CF_G = 12
COL_CF_GATE = 13
COL_SC_B = 14
COL_SC_C = 15
COL_SC_V = 16
COL_SC_GATE = 17
COL_Q = 18
COL_ATT_GATE = 19
COL_K = 80
COL_V = 81
PROJ_COLS = 21504


def _sigmoid(x):
    return 1.0 / (1.0 + jnp.exp(-x))


def _silu(x):
    return x * _sigmoid(x)


def _cparams(sem, limit=VMEM_LIMIT):
    return pltpu.CompilerParams(dimension_semantics=sem, vmem_limit_bytes=limit)


def _tile(m, target):
    t = min(m, target)
    while m % t or t % SUBLANES:
        t -= 1
    return t


def _const_spec(shape):
    return pl.BlockSpec(shape, lambda *_: (0,) * len(shape))


def _stacked_out(l, nbatch, nb, tail, prev):
    shape = (DEPTH, nbatch) + tail
    spec = pl.BlockSpec((1, nb) + tail, lambda b, *_: (l, b) + (0,) * len(tail))
    return jax.ShapeDtypeStruct(shape, F32), spec, ([] if prev is None else [prev])


def _rmsnorm_kernel(x_ref, w_ref, o_ref, *, eps):
    x = x_ref[...].astype(F32)
    ms = jnp.mean(x * x, axis=-1, keepdims=True)
    o_ref[...] = (x * lax.rsqrt(ms + eps) * w_ref[...]).astype(o_ref.dtype)


def rmsnorm(x, w, eps, out_dtype):
    m, d = x.shape
    tm = _tile(m, 1024)
    return pl.pallas_call(
        functools.partial(_rmsnorm_kernel, eps=eps),
        grid=(m // tm,),
        in_specs=[pl.BlockSpec((tm, d), lambda i: (i, 0)), _const_spec((1, d))],
        out_specs=pl.BlockSpec((tm, d), lambda i: (i, 0)),
        out_shape=jax.ShapeDtypeStruct((m, d), out_dtype),
        compiler_params=_cparams(("parallel",)),
        name="rmsnorm",
    )(x, w.reshape(1, d))


_SEG_START = dict(z=0, xs=1024, bc=2048, dt=3072, cf_a=3088, cf_g=4112, cf_gate=5136, sc_b=6160,
                  sc_c=7184, sc_v=8208, sc_gate=9232, q=10256, k=11280, att_gate=11792, merge=12816)
PROJ_TN = 1024


def _proj_block_starts():
    starts = [_SEG_START["merge"] + PROJ_TN * j for j in range(N_BRANCH * D_MODEL // PROJ_TN)]
    starts += [_SEG_START[name] for name in ("z", "xs", "bc", "cf_a", "cf_g", "cf_gate", "sc_b", "sc_c",
                                              "sc_v", "sc_gate", "q", "att_gate")]
    starts.append(_SEG_START["k"])
    assert len(starts) * PROJ_TN == PROJ_COLS and all(v % SUBLANES == 0 for v in starts)
    return jnp.array([v // SUBLANES for v in starts], jnp.int32)


def _inproj_kernel(starts_ref, a_ref, w_ref, o_ref):
    del starts_ref
    w = w_ref[0].astype(BF16)
    o_ref[...] = lax.dot_general(a_ref[...], w, (((1,), (1,)), ((), ())),
                                 preferred_element_type=F32).astype(o_ref.dtype)


def in_proj(xn, w_t, l, tm):
    m, k = xn.shape
    tm = _tile(m, tm)
    grid_spec = pltpu.PrefetchScalarGridSpec(
        num_scalar_prefetch=1, grid=(m // tm, PROJ_COLS // PROJ_TN),
        in_specs=[pl.BlockSpec((tm, k), lambda i, j, s: (i, 0)),
                  pl.BlockSpec((pl.Element(1), pl.Element(PROJ_TN), pl.Element(k)),
                               lambda i, j, s: (l, s[j] * SUBLANES, 0))],
        out_specs=pl.BlockSpec((tm, PROJ_TN), lambda i, j, s: (i, j)))
    return pl.pallas_call(
        _inproj_kernel, grid_spec=grid_spec,
        out_shape=jax.ShapeDtypeStruct((m, PROJ_COLS), BF16),
        compiler_params=_cparams(("parallel", "arbitrary")),
        name="in_proj",
    )(_proj_block_starts(), xn, w_t)


def _merge_out_kernel(*refs, eps, last):
    b_refs, w_refs, m_refs = refs[0:4], refs[4:8], refs[8:12]
    wo_ref, x_ref, nw_ref = refs[12:15]
    out_refs = refs[15:]
    acc = None
    for b_ref, w_ref, m_ref in zip(b_refs, w_refs, m_refs):
        out = jnp.dot(b_ref[...], w_ref[0], preferred_element_type=F32)
        term = _sigmoid(m_ref[...].astype(F32)) * out
        acc = term if acc is None else acc + term
    x = x_ref[...] + jnp.dot(acc.astype(BF16), wo_ref[0], preferred_element_type=F32)
    ms = jnp.mean(x * x, axis=-1, keepdims=True)
    xn = x * lax.rsqrt(ms + eps) * nw_ref[...]
    if last:
        out_refs[0][...] = xn
    else:
        out_refs[0][...] = x
        out_refs[1][...] = xn.astype(out_refs[1].dtype)


def merge_out_norm(branches, w_outs, proj, w_o, x, norm_w, l, eps, last, tm):
    m = x.shape[0]
    tm = _tile(m, tm)
    row = lambda i: (i, 0)
    once = dict(pipeline_mode=pl.Buffered(1))
    merge_specs = [pl.BlockSpec((tm, D_MODEL), functools.partial(lambda i, b: (i, b), b=b))
                   for b in range(N_BRANCH)]
    if last:
        out_shape = [jax.ShapeDtypeStruct((m, D_MODEL), F32)]
    else:
        out_shape = [jax.ShapeDtypeStruct((m, D_MODEL), F32), jax.ShapeDtypeStruct((m, D_MODEL), BF16)]
    return pl.pallas_call(
        functools.partial(_merge_out_kernel, eps=eps, last=last),
        grid=(m // tm,),
        in_specs=[pl.BlockSpec((tm, WIDTH), row)] * N_BRANCH
        + [pl.BlockSpec((1, WIDTH, D_MODEL), lambda i: (l, 0, 0), **once)] * N_BRANCH
        + merge_specs
        + [pl.BlockSpec((1, D_MODEL, D_MODEL), lambda i: (l, 0, 0), **once),
           pl.BlockSpec((tm, D_MODEL), row), _const_spec((1, D_MODEL))],
        out_specs=[pl.BlockSpec((tm, D_MODEL), row)] * len(out_shape),
        out_shape=out_shape,
        compiler_params=_cparams(("parallel",)),
        name="merge_out_norm",
    )(*branches, *w_outs, proj, proj, proj, proj, w_o, x, norm_w.reshape(1, D_MODEL))


def _pad_rows(x, rows):
    if x.shape[0] == rows:
        return x
    return jnp.concatenate([x, jnp.zeros((rows - x.shape[0], x.shape[1]), x.dtype)], axis=0)


def _bf16_terms(x, n):
    terms = []
    for _ in range(n):
        t = x.astype(BF16)
        terms.append(t)
        x = x - t.astype(F32)
    return terms


def _head_stats(dtr_all, dtb, alog, tri3, e2, *, L, nb):
    dtr_all = dtr_all + dtb
    dt_all = jnp.maximum(dtr_all, 0.0) + jnp.log1p(jnp.exp(-jnp.abs(dtr_all)))
    da_all = dt_all * (-jnp.exp(alog))
    css, ecss, decs = [], [], []
    for bi in range(nb):
        da = da_all[bi * L:(bi + 1) * L, :]
        if L == SSM_CHUNK:
            cs = jnp.dot(tri3, jnp.concatenate(_bf16_terms(da, 3), axis=0), preferred_element_type=F32)
        else:
            row = lax.broadcasted_iota(jnp.int32, (L, LANES), 0)
            cs = jnp.zeros((L, LANES), F32)
            for s in range(L):
                cs = cs + jnp.where(row >= s, da[s:s + 1, :], 0.0)
        css.append(cs)
        ecss.append(jnp.exp(cs))
        decs.append(jnp.exp(cs[L - 1:L, :] - cs))
    stack = jnp.concatenate([dt_all] + ecss + decs, axis=0)
    ex = jnp.dot(jnp.concatenate(_bf16_terms(stack, 2), axis=1), e2, preferred_element_type=F32)
    n = nb * L
    return [(css[bi], ex[bi * L:(bi + 1) * L], ex[n + bi * L:n + (bi + 1) * L],
             ex[2 * n + bi * L:2 * n + (bi + 1) * L]) for bi in range(nb)]


def _ssd_core(xbc, z, stats, h_ref, dsk, nw, *, L):
    cs, dt_e, ecs_e, dec_e = stats
    mm = BF16 if L >= 16 else F32
    xs = xbc[:, 0:WIDTH]
    bmat = xbc[:, WIDTH:WIDTH + SSM_GROUPS * SSM_STATE]
    cmat = xbc[:, WIDTH + SSM_GROUPS * SSM_STATE:]
    xdt = xs * dt_e
    xdd = xdt * dec_e

    lp = SSM_CHUNK
    xdt_p = _pad_rows(xdt, lp).astype(mm)
    xdd_p = _pad_rows(xdd, lp)
    b_p = _pad_rows(bmat, lp).astype(mm)
    cs_t = _pad_rows(cs, lp).T
    cmat_m = cmat.astype(mm)

    li = lax.broadcasted_iota(jnp.int32, (L, lp), 0)
    sj = lax.broadcasted_iota(jnp.int32, (L, lp), 1)
    causal = sj <= li
    lane = lax.broadcasted_iota(jnp.int32, (lp, LANES), 1)
    lo_half = lane < SSM_HEADDIM

    y_blocks = []
    hpg = SSM_HEADS // SSM_GROUPS
    gw = hpg * SSM_HEADDIM
    bpg = gw // LANES
    for g in range(SSM_GROUPS):
        c_g = cmat_m[:, g * SSM_STATE:(g + 1) * SSM_STATE]
        b_g = b_p[:, g * SSM_STATE:(g + 1) * SSM_STATE]
        cb = lax.dot_general(c_g, b_g, (((1,), (1,)), ((), ())), preferred_element_type=F32)
        h_blk = h_ref[g * gw:(g + 1) * gw, :]
        y_off = lax.dot_general(c_g, h_blk.astype(mm), (((1,), (1,)), ((), ())),
                                preferred_element_type=F32)
        y_off = y_off * ecs_e[:, g * gw:(g + 1) * gw]
        for jb in range(bpg):
            blk = g * bpg + jb
            x_blk = xdt_p[:, blk * LANES:(blk + 1) * LANES]
            acc = None
            for par in range(2):
                h = 2 * blk + par
                diff = cs[:, h:h + 1] - cs_t[h:h + 1, :]
                lm = jnp.exp(jnp.where(causal, diff, -jnp.inf))
                gmat = (cb * lm).astype(mm)
                x_half = jnp.where(lo_half if par == 0 else jnp.logical_not(lo_half), x_blk,
                                   jnp.zeros_like(x_blk))
                t = jnp.dot(gmat, x_half, preferred_element_type=F32)
                acc = t if acc is None else acc + t
            y_blocks.append(acc + y_off[:, jb * LANES:(jb + 1) * LANES])
        xdd_t = jnp.concatenate(
            [xdd_p[:, (g * bpg + jb) * LANES:(g * bpg + jb + 1) * LANES].T for jb in range(bpg)],
            axis=0).astype(mm)
        s_new = jnp.dot(xdd_t, b_g, preferred_element_type=F32)
        for r in range(hpg):
            h = g * hpg + r
            scal = jnp.exp(cs_t[h:h + 1, L - 1:L])
            rows = slice(h * SSM_HEADDIM, (h + 1) * SSM_HEADDIM)
            h_ref[rows, :] = h_ref[rows, :] * scal + s_new[r * SSM_HEADDIM:(r + 1) * SSM_HEADDIM, :]

    y = jnp.concatenate(y_blocks, axis=1) + dsk * xs
    y = y * _silu(z)
    ms = jnp.mean(y * y, axis=-1, keepdims=True)
    return y * lax.rsqrt(ms + 1e-5) * nw


def _ssd_kernel(*refs, L, nc, nb, zero_state, n_alias):
    xs_ref, bc_ref, z_ref, xn_ref, wdt_ref = refs[:5]
    refs = refs[5:]
    if not zero_state:
        pre_ref, h0_ref = refs[:2]
        refs = refs[2:]
    cw_ref, cb_ref, dtb_ref, alog_ref, dsk_ref, nw_ref, tri_ref, e_ref = refs[:8]
    y_ref, hout_ref, cst_ref, xp_ref, h_ref = refs[8 + n_alias:]
    c = pl.program_id(1)
    off = SUBLANES - (SSM_CONV - 1)

    dtr_all = lax.dot_general(xn_ref[...], wdt_ref[0].astype(BF16), (((1,), (1,)), ((), ())),
                              preferred_element_type=F32)
    stats = _head_stats(dtr_all, dtb_ref[...], alog_ref[...], tri_ref[...], e_ref[...], L=L, nb=nb)
    xs_all = xs_ref[...].astype(F32)
    bc_all = bc_ref[...].astype(F32)
    z_all = z_ref[...].astype(F32)
    ys = []
    for bi in range(nb):
        rows = slice(bi * L, (bi + 1) * L)

        @pl.when(c == 0)
        def _():
            if zero_state:
                xp_ref[bi, off:SUBLANES, :] = jnp.zeros((SSM_CONV - 1, 2 * WIDTH), F32)
                h_ref[bi] = jnp.zeros((WIDTH, SSM_STATE), F32)
            else:
                xp_ref[bi, off:SUBLANES, :] = pre_ref[0, bi]
                h_ref[bi] = h0_ref[0, bi]

        if nc > 1:
            @pl.when(c > 0)
            def _():
                xp_ref[bi, off:SUBLANES, :] = xp_ref[bi, L + off:L + SUBLANES, :]

        xp_ref[bi, SUBLANES:SUBLANES + L, 0:WIDTH] = xs_all[rows, :]
        xp_ref[bi, SUBLANES:SUBLANES + L, WIDTH:2 * WIDTH] = bc_all[rows, :]

        conv = cb_ref[...] + cw_ref[0:1, :] * xp_ref[bi, off:off + L, :]
        for k in range(1, SSM_CONV):
            conv = conv + cw_ref[k:k + 1, :] * xp_ref[bi, off + k:off + k + L, :]
        ys.append(_ssd_core(_silu(conv), z_all[rows, :], stats[bi], h_ref.at[bi], dsk_ref[...],
                            nw_ref[...], L=L))

        @pl.when(c == nc - 1)
        def _():
            hout_ref[0, bi] = h_ref[bi]
            cst_ref[0, bi] = xp_ref[bi, L + off:L + SUBLANES, :]

    y_ref[...] = jnp.concatenate(ys, axis=0).astype(y_ref.dtype)


def ssd_branch(proj, xn, w_t, lp, consts, l, *, nbatch, t, conv_state=None, ssm_state=None, prev=None):
    L = SSM_CHUNK if t % SSM_CHUNK == 0 else t
    nc = t // L
    zero_state = conv_state is None
    nb = 8 if nc == 1 and nbatch % 8 == 0 else 1
    m = nbatch * t
    rl = nb * L

    def rows(b, c):
        return b * nc + c

    def col(cidx):
        return pl.BlockSpec((rl, WIDTH), lambda b, c: (rows(b, c), cidx))

    tri, e = consts
    in_specs = [col(COL_XS), col(COL_BC), col(COL_Z),
                pl.BlockSpec((rl, D_MODEL), lambda b, c: (rows(b, c), 0)),
                pl.BlockSpec((1, LANES, D_MODEL), lambda b, c: (l, _SEG_START["dt"] // LANES, 0))]
    args = [proj, proj, proj, xn, w_t]
    if not zero_state:
        in_specs += [pl.BlockSpec((1, nb, SSM_CONV - 1, 2 * WIDTH), lambda b, c: (l, b, 0, 0)),
                     pl.BlockSpec((1, nb, WIDTH, SSM_STATE), lambda b, c: (l, b, 0, 0))]
        args += [conv_state, ssm_state]
    in_specs += [_const_spec((SSM_CONV, 2 * WIDTH)), _const_spec((1, 2 * WIDTH)),
                 _const_spec((1, LANES)), _const_spec((1, LANES)), _const_spec((1, WIDTH)),
                 _const_spec((1, WIDTH)), _const_spec((SSM_CHUNK, 3 * SSM_CHUNK)),
                 _const_spec((2 * LANES, WIDTH))]
    args += [lp["ssm_conv_w"], lp["ssm_conv_b"], lp["dt_bias"], lp["a_log"], lp["d_skip"],
             lp["ssm_norm_w"], tri, e]
    h_shape, h_spec, h_prev = _stacked_out(l, nbatch, nb, (WIDTH, SSM_STATE), prev and prev[0])
    c_shape, c_spec, c_prev = _stacked_out(l, nbatch, nb, (SSM_CONV - 1, 2 * WIDTH), prev and prev[1])
    alias_in = h_prev + c_prev
    aliases = {len(args) + i: 1 + i for i in range(len(alias_in))}
    in_specs += [pl.BlockSpec(memory_space=pl.ANY)] * len(alias_in)
    y, h_out, c_out = pl.pallas_call(
        functools.partial(_ssd_kernel, L=L, nc=nc, nb=nb, zero_state=zero_state, n_alias=len(alias_in)),
        grid=(nbatch // nb, nc),
        in_specs=in_specs,
        out_specs=[pl.BlockSpec((rl, WIDTH), lambda b, c: (rows(b, c), 0)), h_spec, c_spec],
        out_shape=[jax.ShapeDtypeStruct((m, WIDTH), BF16), h_shape, c_shape],
        scratch_shapes=[pltpu.VMEM((nb, SUBLANES + L, 2 * WIDTH), F32),
                        pltpu.VMEM((nb, WIDTH, SSM_STATE), F32)],
        input_output_aliases=aliases,
        compiler_params=_cparams(("parallel", "arbitrary")),
        name="ssd_branch",
    )(*args, *alias_in)
    return y, (h_out, c_out)


def _dwconv(xp_ref, xsh_ref, w_ref, b_ref, acc_ref, *, k, start, tb, row0):
    rbk = min(tb, 32)
    if xsh_ref is not None:
        nrows = xsh_ref.shape[1]
        for s in range(1, SUBLANES):
            xsh_ref[s - 1] = xp_ref[s:s + nrows, :]
    for j in range(WIDTH // LANES):
        lanes = slice(j * LANES, (j + 1) * LANES)
        for r in range(tb // rbk):
            r0 = r * rbk
            acc = None
            for i in range(k):
                first = start + i + r0
                s = first % SUBLANES
                if xsh_ref is None or s == 0:
                    x = xp_ref[first:first + rbk, lanes]
                else:
                    x = xsh_ref[s - 1, first - s:first - s + rbk, lanes]
                term = w_ref[i:i + 1, lanes] * x
                if acc is None:
                    acc = term if b_ref is None else term + b_ref[:, lanes]
                else:
                    acc = acc + term
            acc_ref[row0 + r0:row0 + r0 + rbk, lanes] = acc


def _conv_all(u, xp_ref, pre_ref, st_ref, w_ref, b_ref, acc_ref, xsh_ref=None, *, k, p, tb, nc, nb):
    c = pl.program_id(1)
    for bi in range(nb):
        @pl.when(c == 0)
        def _():
            if pre_ref is None:
                xp_ref[bi, p - (k - 1):p, :] = jnp.zeros((k - 1, WIDTH), F32)
            else:
                xp_ref[bi, p - (k - 1):p, :] = pre_ref[0, bi]

        if nc > 1:
            @pl.when(c > 0)
            def _():
                xp_ref[bi, 0:p, :] = xp_ref[bi, tb:tb + p, :]

        xp_ref[bi, p:p + tb, :] = u[bi * tb:(bi + 1) * tb, :]
        _dwconv(xp_ref.at[bi], xsh_ref, w_ref, b_ref, acc_ref, k=k, start=p - (k - 1), tb=tb,
                row0=bi * tb)

        @pl.when(c == nc - 1)
        def _():
            st_ref[0, bi] = xp_ref[bi, p + tb - (k - 1):p + tb, :]


def _cf_kernel(*refs, tb, nc, nb, p, zero_state, n_alias):
    a_ref, g_ref, gate_ref = refs[:3]
    refs = refs[3:]
    pre_ref = None
    if not zero_state:
        pre_ref, refs = refs[0], refs[1:]
    w_ref, b_ref, lnw_ref, lnb_ref = refs[:4]
    o_ref, st_ref, xp_ref, acc_ref, *xsh = refs[4 + n_alias:]
    u = a_ref[...].astype(F32) * _sigmoid(g_ref[...].astype(F32))
    _conv_all(u, xp_ref, pre_ref, st_ref, w_ref, b_ref, acc_ref, *xsh, k=CF_CONV, p=p, tb=tb, nc=nc,
              nb=nb)
    u = acc_ref[...]
    mu = jnp.mean(u, axis=-1, keepdims=True)
    uc = u - mu
    var = jnp.mean(uc * uc, axis=-1, keepdims=True)
    u = _silu(uc * lax.rsqrt(var + 1e-5) * lnw_ref[...] + lnb_ref[...])
    o_ref[...] = (u * _silu(gate_ref[...].astype(F32))).astype(o_ref.dtype)


def _sc_kernel(*refs, tb, nc, nb, p, zero_state, n_alias):
    gb_ref, gc_ref, sv_ref, gate_ref = refs[:4]
    refs = refs[4:]
    pre_ref = None
    if not zero_state:
        pre_ref, refs = refs[0], refs[1:]
    w_ref = refs[0]
    o_ref, st_ref, xp_ref, acc_ref = refs[1 + n_alias:]
    u = gc_ref[...].astype(F32) * sv_ref[...].astype(F32)
    _conv_all(u, xp_ref, pre_ref, st_ref, w_ref, None, acc_ref, k=SC_CONV, p=p, tb=tb, nc=nc, nb=nb)
    o_ref[...] = (gb_ref[...].astype(F32) * acc_ref[...]
                  * _silu(gate_ref[...].astype(F32))).astype(o_ref.dtype)


def conv_branch(kind, proj, lp, l, *, nbatch, t, state=None, prev=None):
    tb = min(t, 512)
    nc = t // tb
    nb = 8 if nc == 1 and nbatch % 8 == 0 else 1
    m = nbatch * t
    rl = nb * tb
    zero_state = state is None
    k = CF_CONV if kind == "cf" else SC_CONV
    p = -(-(k - 1) // SUBLANES) * SUBLANES

    def rows(b, c):
        return b * nc + c

    def col(cidx):
        return pl.BlockSpec((rl, WIDTH), lambda b, c: (rows(b, c), cidx))

    if kind == "cf":
        in_specs = [col(COL_CF_A), col(COL_CF_G), col(COL_CF_GATE)]
        args = [proj, proj, proj]
        w_specs = [_const_spec((k, WIDTH)), _const_spec((1, WIDTH)), _const_spec((1, WIDTH)),
                   _const_spec((1, WIDTH))]
        w_args = [lp["cf_conv_w"], lp["cf_conv_b"], lp["cf_ln_w"], lp["cf_ln_b"]]
        body = _cf_kernel
    else:
        in_specs = [col(COL_SC_B), col(COL_SC_C), col(COL_SC_V), col(COL_SC_GATE)]
        args = [proj, proj, proj, proj]
        w_specs = [_const_spec((k, WIDTH))]
        w_args = [lp["sc_conv_w"]]
        body = _sc_kernel
    if not zero_state:
        in_specs.append(pl.BlockSpec((1, nb, k - 1, WIDTH), lambda b, c: (l, b, 0, 0)))
        args.append(state)
    in_specs += w_specs
    args += w_args
    s_shape, s_spec, alias_in = _stacked_out(l, nbatch, nb, (k - 1, WIDTH), prev)
    aliases = {len(args) + i: 1 + i for i in range(len(alias_in))}
    in_specs += [pl.BlockSpec(memory_space=pl.ANY)] * len(alias_in)
    scratch = [pltpu.VMEM((nb, p + tb, WIDTH), F32), pltpu.VMEM((rl, WIDTH), F32)]
    if kind == "cf" and nb == 1 and tb % 32 == 0:
        scratch.append(pltpu.VMEM((SUBLANES - 1, p + tb - SUBLANES, WIDTH), F32))
    return pl.pallas_call(
        functools.partial(body, tb=tb, nc=nc, nb=nb, p=p, zero_state=zero_state, n_alias=len(alias_in)),
        grid=(nbatch // nb, nc),
        in_specs=in_specs,
        out_specs=[pl.BlockSpec((rl, WIDTH), lambda b, c: (rows(b, c), 0)), s_spec],
        out_shape=[jax.ShapeDtypeStruct((m, WIDTH), BF16), s_shape],
        scratch_shapes=scratch,
        input_output_aliases=aliases,
        compiler_params=_cparams(("parallel", "arbitrary")),
        name=kind + "_branch",
    )(*args, *alias_in)


def _sink_column(sink_ref, g, rows_per_head):
    n = ATT_REP * rows_per_head
    rid = lax.broadcasted_iota(jnp.int32, (n, 1), 0)
    col = jnp.full((n, 1), sink_ref[g * ATT_REP + ATT_REP - 1], F32)
    for r in range(ATT_REP - 2, -1, -1):
        col = jnp.where(rid < (r + 1) * rows_per_head, sink_ref[g * ATT_REP + r], col)
    return col


def _softmax_sink(s, ok, sink_col):
    s = jnp.where(ok, s, -jnp.inf)
    m = jnp.maximum(jnp.max(s, axis=-1, keepdims=True), sink_col)
    p = jnp.exp(s - m)
    denom = jnp.sum(p, axis=-1, keepdims=True) + jnp.exp(sink_col - m)
    return p / denom


def _attn_prompt_kernel(q_ref, kp_ref, kc_ref, vp_ref, vc_ref, gate_ref, sink_ref, *rest, nblk, n_alias):
    o_ref, kw_ref, vw_ref = rest[n_alias:]
    L = WINDOW
    i = pl.program_id(1)
    q = q_ref[...].astype(F32) * (ATT_HEAD_DIM ** -0.5)
    k2 = jnp.concatenate([kp_ref[...], kc_ref[...]], axis=0).astype(F32)
    v2 = jnp.concatenate([vp_ref[...], vc_ref[...]], axis=0).astype(F32)
    li = lax.broadcasted_iota(jnp.int32, (ATT_REP * L, 2 * L), 0) & (L - 1)
    sj = lax.broadcasted_iota(jnp.int32, (ATT_REP * L, 2 * L), 1)
    ok = (sj > li) & (sj <= li + WINDOW) & ((sj >= L) | (i > 0))
    lane = lax.broadcasted_iota(jnp.int32, (2 * L, LANES), 1)
    lo_half = lane < ATT_HEAD_DIM

    o_blocks = [None] * (WIDTH // LANES)
    for g in range(ATT_KV_HEADS):
        half = g % 2
        sl = slice((g // 2) * LANES, (g // 2 + 1) * LANES)
        keep = lo_half if half == 0 else jnp.logical_not(lo_half)
        k_use = jnp.where(keep, k2[:, sl], 0.0).astype(BF16)
        v_use = jnp.where(keep, v2[:, sl], 0.0).astype(BF16)
        parts = []
        for r in range(ATT_REP):
            h = g * ATT_REP + r
            qb = q[:, (h // 2) * LANES:(h // 2 + 1) * LANES]
            parts.append(qb if h % 2 == half else pltpu.roll(qb, ATT_HEAD_DIM, 1))
        lhs = jnp.concatenate(parts, axis=0).astype(BF16)
        s = lax.dot_general(lhs, k_use, (((1,), (1,)), ((), ())), preferred_element_type=F32)
        pn = _softmax_sink(s, ok, _sink_column(sink_ref, g, L)).astype(BF16)
        o = jnp.dot(pn, v_use, preferred_element_type=F32)
        for r in range(ATT_REP):
            h = g * ATT_REP + r
            ob = o[r * L:(r + 1) * L, :]
            if h % 2 != half:
                ob = pltpu.roll(ob, ATT_HEAD_DIM, 1)
            o_blocks[h // 2] = ob if o_blocks[h // 2] is None else o_blocks[h // 2] + ob
    o = jnp.concatenate(o_blocks, axis=1)
    o_ref[...] = (o * _silu(gate_ref[...].astype(F32))).astype(o_ref.dtype)

    @pl.when(i == nblk - 1)
    def _():
        kw_ref[0, 0] = kc_ref[...].astype(F32).T.reshape(ATT_KV_HEADS, ATT_HEAD_DIM, WINDOW)
        vw_ref[0, 0] = vc_ref[...].astype(F32).T.reshape(ATT_KV_HEADS, ATT_HEAD_DIM, WINDOW)


def _win_out(l, nbatch, nb, prev):
    return _stacked_out(l, nbatch, nb, (ATT_KV_HEADS, ATT_HEAD_DIM, WINDOW), prev)


def attn_prompt(proj, sinks, l, *, nbatch, t, prev=None):
    L = WINDOW
    nblk = t // L

    def col(cidx):
        return pl.BlockSpec((L, WIDTH), lambda b, i: (b * nblk + i, cidx))

    def kv(cidx, back):
        return pl.BlockSpec((L, ATT_KV_WIDTH), lambda b, i: (b * nblk + jnp.maximum(i - back, 0), cidx))

    k_shape, k_spec, k_prev = _win_out(l, nbatch, 1, prev and prev[0])
    v_shape, v_spec, v_prev = _win_out(l, nbatch, 1, prev and prev[1])
    alias_in = k_prev + v_prev
    args = [proj, proj, proj, proj, proj, proj, sinks]
    aliases = {len(args) + i: 1 + i for i in range(len(alias_in))}
    y, kw, vw = pl.pallas_call(
        functools.partial(_attn_prompt_kernel, nblk=nblk, n_alias=len(alias_in)),
        grid=(nbatch, nblk),
        in_specs=[col(COL_Q), kv(COL_K, 1), kv(COL_K, 0), kv(COL_V, 1), kv(COL_V, 0),
                  col(COL_ATT_GATE), pl.BlockSpec(memory_space=pltpu.SMEM)]
        + [pl.BlockSpec(memory_space=pl.ANY)] * len(alias_in),
        out_specs=[pl.BlockSpec((L, WIDTH), lambda b, i: (b * nblk + i, 0)), k_spec, v_spec],
        out_shape=[jax.ShapeDtypeStruct((nbatch * t, WIDTH), BF16), k_shape, v_shape],
        input_output_aliases=aliases,
        compiler_params=_cparams(("parallel", "arbitrary")),
        name="attn_prompt",
    )(*args, *alias_in)
    return y, (kw, vw)


def _attn_sample_kernel(q_ref, kn_ref, vn_ref, ck_ref, cv_ref, gate_ref, sink_ref, *rest, L, nb, n_alias):
    o_ref, kw_ref, vw_ref = rest[n_alias:]
    q = q_ref[...].astype(F32) * (ATT_HEAD_DIM ** -0.5)
    kn_t = _pad_rows(kn_ref[...].astype(F32), WINDOW).T
    vn_t = _pad_rows(vn_ref[...].astype(F32), WINDOW).T
    nkeys = 2 * WINDOW
    rq = ATT_REP * L
    row = lax.broadcasted_iota(jnp.int32, (nb * rq, nkeys), 0)
    sj = lax.broadcasted_iota(jnp.int32, (nb * rq, nkeys), 1)
    li = row & (L - 1)
    cj = sj - WINDOW - (row // rq) * L
    ok = ((sj < WINDOW) & (sj > li)) | ((cj >= 0) & (cj <= li))
    head_of_row = (lax.broadcasted_iota(jnp.int32, (nb * rq, 1), 0) // L) % ATT_REP
    lane = lax.broadcasted_iota(jnp.int32, (ATT_HEAD_DIM, WINDOW), 1)
    is_new = lane >= WINDOW - L
    pieces = [[None] * ATT_HEADS for _ in range(nb)]
    for g in range(ATT_KV_HEADS):
        gl = slice(g * ATT_HEAD_DIM, (g + 1) * ATT_HEAD_DIM)
        kn_g, vn_g = kn_t[gl, :], vn_t[gl, :]
        scores, values = [], []
        for bi in range(nb):
            rows = slice(bi * L, (bi + 1) * L)
            kc, vc = ck_ref[0, bi, g], cv_ref[0, bi, g]
            k2 = jnp.concatenate([kc, kn_g], axis=1).astype(BF16)
            values.append(jnp.concatenate([vc, vn_g], axis=1).astype(BF16))
            lhs = jnp.concatenate(
                [q[rows, (g * ATT_REP + r) * ATT_HEAD_DIM:(g * ATT_REP + r + 1) * ATT_HEAD_DIM]
                 for r in range(ATT_REP)], axis=0).astype(BF16)
            scores.append(jnp.dot(lhs, k2, preferred_element_type=F32))
            shift_new = (WINDOW - L - bi * L) % WINDOW
            for c_old, n_g, w_ref in ((kc, kn_g, kw_ref), (vc, vn_g, vw_ref)):
                tail = n_g if shift_new == 0 else pltpu.roll(n_g, shift_new, 1)
                w_ref[0, bi, g] = jnp.where(is_new, tail, pltpu.roll(c_old, WINDOW - L, 1))
        sink_col = jnp.full((nb * rq, 1), sink_ref[g * ATT_REP], F32)
        for r in range(1, ATT_REP):
            sink_col = jnp.where(head_of_row == r, sink_ref[g * ATT_REP + r], sink_col)
        pn = _softmax_sink(jnp.concatenate(scores, axis=0), ok, sink_col).astype(BF16)
        for bi in range(nb):
            o = lax.dot_general(pn[bi * rq:(bi + 1) * rq, :], values[bi], (((1,), (1,)), ((), ())),
                                preferred_element_type=F32)
            for r in range(ATT_REP):
                pieces[bi][g * ATT_REP + r] = o[r * L:(r + 1) * L, :]
    o = jnp.concatenate([jnp.concatenate(p, axis=1) for p in pieces], axis=0)
    o_ref[...] = (o * _silu(gate_ref[...].astype(F32))).astype(o_ref.dtype)


def attn_sample(proj, sinks, cache_k, cache_v, l, *, nbatch, t, prev=None):
    nb = WINDOW // t if nbatch % (WINDOW // t) == 0 else nbatch
    rl = nb * t

    def col(cidx, w):
        return pl.BlockSpec((rl, w), lambda b: (b, cidx))

    cache = pl.BlockSpec((1, nb, ATT_KV_HEADS, ATT_HEAD_DIM, WINDOW), lambda b: (l, b, 0, 0, 0))
    k_shape, k_spec, k_prev = _win_out(l, nbatch, nb, prev and prev[0])
    v_shape, v_spec, v_prev = _win_out(l, nbatch, nb, prev and prev[1])
    alias_in = k_prev + v_prev
    args = [proj, proj, proj, cache_k, cache_v, proj, sinks]
    aliases = {len(args) + i: 1 + i for i in range(len(alias_in))}
    y, kw, vw = pl.pallas_call(
        functools.partial(_attn_sample_kernel, L=t, nb=nb, n_alias=len(alias_in)),
        grid=(nbatch // nb,),
        in_specs=[col(COL_Q, WIDTH), col(COL_K, ATT_KV_WIDTH), col(COL_V, ATT_KV_WIDTH), cache, cache,
                  col(COL_ATT_GATE, WIDTH), pl.BlockSpec(memory_space=pltpu.SMEM)]
        + [pl.BlockSpec(memory_space=pl.ANY)] * len(alias_in),
        out_specs=[pl.BlockSpec((rl, WIDTH), lambda b: (b, 0)), k_spec, v_spec],
        out_shape=[jax.ShapeDtypeStruct((nbatch * t, WIDTH), BF16), k_shape, v_shape],
        input_output_aliases=aliases,
        compiler_params=_cparams(("parallel",)),
        name="attn_sample",
    )(*args, *alias_in)
    return y, (kw, vw)


def _prep_layer(l, norm_w, w_in, ssm_conv_w, ssm_conv_b, ssm_dt_bias, ssm_a_log, ssm_d, ssm_norm_w,
                w_out_ssm, cf_conv_w, cf_conv_b, cf_ln_w, cf_ln_b, w_out_cf, sc_conv_w, w_out_sc,
                att_sinks, w_out_att, w_o):
    del w_in
    pad_h = (0, LANES - SSM_HEADS)
    return dict(
        norm_w=norm_w[l],
        ssm_conv_w=ssm_conv_w[l], ssm_conv_b=ssm_conv_b[l].reshape(1, -1),
        dt_bias=jnp.pad(ssm_dt_bias[l], pad_h).reshape(1, LANES),
        a_log=jnp.pad(ssm_a_log[l], pad_h).reshape(1, LANES),
        d_skip=jnp.repeat(ssm_d[l], SSM_HEADDIM).reshape(1, WIDTH),
        ssm_norm_w=ssm_norm_w[l].reshape(1, WIDTH),
        cf_conv_w=cf_conv_w[l], cf_conv_b=cf_conv_b[l].reshape(1, WIDTH),
        cf_ln_w=cf_ln_w[l].reshape(1, WIDTH), cf_ln_b=cf_ln_b[l].reshape(1, WIDTH),
        sc_conv_w=sc_conv_w[l], sinks=att_sinks[l],
        w_outs=tuple(w.astype(BF16) for w in (w_out_ssm, w_out_cf, w_out_sc, w_out_att)),
        w_o=w_o.astype(BF16),
    )


def _layer(x, xn, w_t, lp, consts, l, states, prev, next_norm_w, *, nbatch, t):
    kw = dict(nbatch=nbatch, t=t)
    proj = in_proj(xn, w_t, l, tm=2048)
    if states is None:
        ya, st_a = ssd_branch(proj, xn, w_t, lp, consts, l, prev=prev and prev[0], **kw)
        yb, st_b = conv_branch("cf", proj, lp, l, prev=prev and prev[1], **kw)
        yc, st_c = conv_branch("sc", proj, lp, l, prev=prev and prev[2], **kw)
        yd, st_d = attn_prompt(proj, lp["sinks"], l, prev=prev and prev[3], **kw)
    else:
        ssm, conv_ssm, conv_cf, conv_sc, cache_k, cache_v = states
        ya, st_a = ssd_branch(proj, xn, w_t, lp, consts, l, conv_state=conv_ssm, ssm_state=ssm,
                              prev=prev and prev[0], **kw)
        yb, st_b = conv_branch("cf", proj, lp, l, state=conv_cf, prev=prev and prev[1], **kw)
        yc, st_c = conv_branch("sc", proj, lp, l, state=conv_sc, prev=prev and prev[2], **kw)
        yd, st_d = attn_sample(proj, lp["sinks"], cache_k, cache_v, l, prev=prev and prev[3], **kw)
    outs = merge_out_norm((ya, yb, yc, yd), lp["w_outs"], proj, lp["w_o"], x, next_norm_w, l, EPS,
                          last=l == DEPTH - 1, tm=256)
    return outs, (st_a, st_b, st_c, st_d)


def kernel(x_prompt, x_sample, state_ssm, state_conv_ssm, state_conv_cf, state_conv_sc, cache_k, cache_v, norm_w, w_in, ssm_conv_w, ssm_conv_b, ssm_dt_bias, ssm_a_log, ssm_d, ssm_norm_w, w_out_ssm, cf_conv_w, cf_conv_b, cf_ln_w, cf_ln_b, w_out_cf, sc_conv_w, w_out_sc, att_sinks, w_out_att, w_o, final_norm_w):
    bp, tp, _ = x_prompt.shape
    bs, ts, _ = x_sample.shape
    xp = x_prompt.reshape(bp * tp, D_MODEL)
    xs = x_sample.reshape(bs * ts, D_MODEL)

    tri = jnp.tril(jnp.ones((SSM_CHUNK, SSM_CHUNK), BF16))
    expand = (jnp.arange(WIDTH)[None, :] // SSM_HEADDIM == jnp.arange(LANES)[:, None]).astype(BF16)
    consts = (jnp.tile(tri, (1, 3)), jnp.tile(expand, (2, 1)))
    to_win_minor = lambda c: jnp.transpose(c, (0, 1, 3, 4, 2))
    from_win_minor = lambda c: jnp.transpose(c, (0, 1, 4, 2, 3))
    w_t = jnp.swapaxes(w_in, 1, 2)
    s_states = (state_ssm.reshape(DEPTH, bs, WIDTH, SSM_STATE), state_conv_ssm, state_conv_cf,
                state_conv_sc, to_win_minor(cache_k), to_win_minor(cache_v))

    xnp = rmsnorm(xp, norm_w[0], EPS, BF16)
    xns = rmsnorm(xs, norm_w[0], EPS, BF16)
    p_prev = s_prev = None
    for l in range(DEPTH):
        lp = _prep_layer(l, norm_w, w_in, ssm_conv_w, ssm_conv_b, ssm_dt_bias, ssm_a_log, ssm_d,
                         ssm_norm_w, w_out_ssm, cf_conv_w, cf_conv_b, cf_ln_w, cf_ln_b, w_out_cf,
                         sc_conv_w, w_out_sc, att_sinks, w_out_att, w_o)
        next_w = norm_w[l + 1] if l + 1 < DEPTH else final_norm_w
        outp, p_prev = _layer(xp, xnp, w_t, lp, consts, l, None, p_prev, next_w, nbatch=bp, t=tp)
        outs, s_prev = _layer(xs, xns, w_t, lp, consts, l, s_states, s_prev, next_w, nbatch=bs, t=ts)
        if l + 1 < DEPTH:
            (xp, xnp), (xs, xns) = outp, outs
    y_prompt = outp[0].reshape(bp, tp, D_MODEL)
    y_sample = outs[0].reshape(bs, ts, D_MODEL)

    def unpack(st, nbatch):
        (h, conv_ssm), conv_cf, conv_sc, (kwin, vwin) = st
        return (h.reshape(DEPTH, nbatch, SSM_HEADS, SSM_HEADDIM, SSM_STATE), conv_ssm, conv_cf,
                conv_sc, from_win_minor(kwin), from_win_minor(vwin))

    return (y_prompt, y_sample, *unpack(p_prev, bp), *unpack(s_prev, bs))
```

```python
import functools

import jax
import jax.numpy as jnp
from jax import lax
from jax.experimental import pallas as pl
from jax.experimental.pallas import tpu as pltpu

F32 = jnp.float32
BF16 = jnp.bfloat16

D_MODEL = 2048
DEPTH = 2
N_BRANCH = 4
WIDTH = 1024
SSM_HEADS = 16
SSM_HEADDIM = 64
SSM_GROUPS = 4
SSM_STATE = 128
SSM_CONV = 4
SSM_CHUNK = 128
CF_CONV = 31
SC_CONV = 3
ATT_HEADS = 16
ATT_KV_HEADS = 4
ATT_HEAD_DIM = 64
ATT_KV_WIDTH = ATT_KV_HEADS * ATT_HEAD_DIM
ATT_REP = ATT_HEADS // ATT_KV_HEADS
WINDOW = 128
EPS = 1e-6

LANES = 128
SUBLANES = 8
VMEM_LIMIT = 56 * 1024 * 1024

COL_MERGE = 0
COL_Z = 8
COL_XS = 9
COL_BC = 10
COL_CF_A = 11
COL_CF_G = 12
COL_CF_GATE = 13
COL_SC_B = 14
COL_SC_C = 15
COL_SC_V = 16
COL_SC_GATE = 17
COL_Q = 18
COL_ATT_GATE = 19
COL_K = 80
COL_V = 81
PROJ_COLS = 21504


def _sigmoid(x):
    return 1.0 / (1.0 + jnp.exp(-x))


def _silu(x):
    return x * _sigmoid(x)


def _cparams(sem, limit=VMEM_LIMIT):
    return pltpu.CompilerParams(dimension_semantics=sem, vmem_limit_bytes=limit)


def _tile(m, target):
    t = min(m, target)
    while m % t or t % SUBLANES:
        t -= 1
    return t


def _const_spec(shape):
    return pl.BlockSpec(shape, lambda *_: (0,) * len(shape))


def _stacked_out(l, nbatch, nb, tail, prev):
    shape = (DEPTH, nbatch) + tail
    spec = pl.BlockSpec((1, nb) + tail, lambda b, *_: (l, b) + (0,) * len(tail))
    return jax.ShapeDtypeStruct(shape, F32), spec, ([] if prev is None else [prev])


def _rmsnorm_kernel(x_ref, w_ref, o_ref, *, eps):
    x = x_ref[...].astype(F32)
    ms = jnp.mean(x * x, axis=-1, keepdims=True)
    o_ref[...] = (x * lax.rsqrt(ms + eps) * w_ref[...]).astype(o_ref.dtype)


def rmsnorm(x, w, eps, out_dtype):
    m, d = x.shape
    tm = _tile(m, 1024)
    return pl.pallas_call(
        functools.partial(_rmsnorm_kernel, eps=eps),
        grid=(m // tm,),
        in_specs=[pl.BlockSpec((tm, d), lambda i: (i, 0)), _const_spec((1, d))],
        out_specs=pl.BlockSpec((tm, d), lambda i: (i, 0)),
        out_shape=jax.ShapeDtypeStruct((m, d), out_dtype),
        compiler_params=_cparams(("parallel",)),
        name="rmsnorm",
    )(x, w.reshape(1, d))


_SEG_START = dict(z=0, xs=1024, bc=2048, dt=3072, cf_a=3088, cf_g=4112, cf_gate=5136, sc_b=6160,
                  sc_c=7184, sc_v=8208, sc_gate=9232, q=10256, k=11280, att_gate=11792, merge=12816)
PROJ_TN = 1024


def _proj_block_starts():
    starts = [_SEG_START["merge"] + PROJ_TN * j for j in range(N_BRANCH * D_MODEL // PROJ_TN)]
    starts += [_SEG_START[name] for name in ("z", "xs", "bc", "cf_a", "cf_g", "cf_gate", "sc_b", "sc_c",
                                              "sc_v", "sc_gate", "q", "att_gate")]
    starts.append(_SEG_START["k"])
    assert len(starts) * PROJ_TN == PROJ_COLS and all(v % SUBLANES == 0 for v in starts)
    return jnp.array([v // SUBLANES for v in starts], jnp.int32)


def _inproj_kernel(starts_ref, a_ref, w_ref, o_ref):
    del starts_ref
    w = w_ref[0].astype(BF16)
    o_ref[...] = lax.dot_general(a_ref[...], w, (((1,), (1,)), ((), ())),
                                 preferred_element_type=F32).astype(o_ref.dtype)


def in_proj(xn, w_t, l, tm):
    m, k = xn.shape
    tm = _tile(m, tm)
    grid_spec = pltpu.PrefetchScalarGridSpec(
        num_scalar_prefetch=1, grid=(m // tm, PROJ_COLS // PROJ_TN),
        in_specs=[pl.BlockSpec((tm, k), lambda i, j, s: (i, 0)),
                  pl.BlockSpec((pl.Element(1), pl.Element(PROJ_TN), pl.Element(k)),
                               lambda i, j, s: (l, s[j] * SUBLANES, 0))],
        out_specs=pl.BlockSpec((tm, PROJ_TN), lambda i, j, s: (i, j)))
    return pl.pallas_call(
        _inproj_kernel, grid_spec=grid_spec,
        out_shape=jax.ShapeDtypeStruct((m, PROJ_COLS), BF16),
        compiler_params=_cparams(("parallel", "arbitrary")),
        name="in_proj",
    )(_proj_block_starts(), xn, w_t)


def _merge_out_kernel(*refs, eps, last):
    b_refs, w_refs, m_refs = refs[0:4], refs[4:8], refs[8:12]
    wo_ref, x_ref, nw_ref = refs[12:15]
    out_refs = refs[15:]
    acc = None
    for b_ref, w_ref, m_ref in zip(b_refs, w_refs, m_refs):
        out = jnp.dot(b_ref[...], w_ref[0], preferred_element_type=F32)
        term = _sigmoid(m_ref[...].astype(F32)) * out
        acc = term if acc is None else acc + term
    x = x_ref[...] + jnp.dot(acc.astype(BF16), wo_ref[0], preferred_element_type=F32)
    ms = jnp.mean(x * x, axis=-1, keepdims=True)
    xn = x * lax.rsqrt(ms + eps) * nw_ref[...]
    if last:
        out_refs[0][...] = xn
    else:
        out_refs[0][...] = x
        out_refs[1][...] = xn.astype(out_refs[1].dtype)


def merge_out_norm(branches, w_outs, proj, w_o, x, norm_w, l, eps, last, tm):
    m = x.shape[0]
    tm = _tile(m, tm)
    row = lambda i: (i, 0)
    once = dict(pipeline_mode=pl.Buffered(1))
    merge_specs = [pl.BlockSpec((tm, D_MODEL), functools.partial(lambda i, b: (i, b), b=b))
                   for b in range(N_BRANCH)]
    if last:
        out_shape = [jax.ShapeDtypeStruct((m, D_MODEL), F32)]
    else:
        out_shape = [jax.ShapeDtypeStruct((m, D_MODEL), F32), jax.ShapeDtypeStruct((m, D_MODEL), BF16)]
    return pl.pallas_call(
        functools.partial(_merge_out_kernel, eps=eps, last=last),
        grid=(m // tm,),
        in_specs=[pl.BlockSpec((tm, WIDTH), row)] * N_BRANCH
        + [pl.BlockSpec((1, WIDTH, D_MODEL), lambda i: (l, 0, 0), **once)] * N_BRANCH
        + merge_specs
        + [pl.BlockSpec((1, D_MODEL, D_MODEL), lambda i: (l, 0, 0), **once),
           pl.BlockSpec((tm, D_MODEL), row), _const_spec((1, D_MODEL))],
        out_specs=[pl.BlockSpec((tm, D_MODEL), row)] * len(out_shape),
        out_shape=out_shape,
        compiler_params=_cparams(("parallel",)),
        name="merge_out_norm",
    )(*branches, *w_outs, proj, proj, proj, proj, w_o, x, norm_w.reshape(1, D_MODEL))


def _pad_rows(x, rows):
    if x.shape[0] == rows:
        return x
    return jnp.concatenate([x, jnp.zeros((rows - x.shape[0], x.shape[1]), x.dtype)], axis=0)


def _bf16_terms(x, n):
    terms = []
    for _ in range(n):
        t = x.astype(BF16)
        terms.append(t)
        x = x - t.astype(F32)
    return terms


def _head_stats(dtr_all, dtb, alog, tri3, e2, *, L, nb):
    dtr_all = dtr_all + dtb
    dt_all = jnp.maximum(dtr_all, 0.0) + jnp.log1p(jnp.exp(-jnp.abs(dtr_all)))
    da_all = dt_all * (-jnp.exp(alog))
    css, ecss, decs = [], [], []
    for bi in range(nb):
        da = da_all[bi * L:(bi + 1) * L, :]
        if L == SSM_CHUNK:
            cs = jnp.dot(tri3, jnp.concatenate(_bf16_terms(da, 3), axis=0), preferred_element_type=F32)
        else:
            row = lax.broadcasted_iota(jnp.int32, (L, LANES), 0)
            cs = jnp.zeros((L, LANES), F32)
            for s in range(L):
                cs = cs + jnp.where(row >= s, da[s:s + 1, :], 0.0)
        css.append(cs)
        ecss.append(jnp.exp(cs))
        decs.append(jnp.exp(cs[L - 1:L, :] - cs))
    stack = jnp.concatenate([dt_all] + ecss + decs, axis=0)
    ex = jnp.dot(jnp.concatenate(_bf16_terms(stack, 2), axis=1), e2, preferred_element_type=F32)
    n = nb * L
    return [(css[bi], ex[bi * L:(bi + 1) * L], ex[n + bi * L:n + (bi + 1) * L],
             ex[2 * n + bi * L:2 * n + (bi + 1) * L]) for bi in range(nb)]


def _ssd_core(xbc, z, stats, h_ref, dsk, nw, *, L):
    cs, dt_e, ecs_e, dec_e = stats
    mm = BF16 if L >= 16 else F32
    xs = xbc[:, 0:WIDTH]
    bmat = xbc[:, WIDTH:WIDTH + SSM_GROUPS * SSM_STATE]
    cmat = xbc[:, WIDTH + SSM_GROUPS * SSM_STATE:]
    xdt = xs * dt_e
    xdd = xdt * dec_e

    lp = SSM_CHUNK
    xdt_p = _pad_rows(xdt, lp).astype(mm)
    xdd_p = _pad_rows(xdd, lp)
    b_p = _pad_rows(bmat, lp).astype(mm)
    cs_t = _pad_rows(cs, lp).T
    cmat_m = cmat.astype(mm)

    li = lax.broadcasted_iota(jnp.int32, (L, lp), 0)
    sj = lax.broadcasted_iota(jnp.int32, (L, lp), 1)
    causal = sj <= li
    lane = lax.broadcasted_iota(jnp.int32, (lp, LANES), 1)
    lo_half = lane < SSM_HEADDIM

    y_blocks = []
    hpg = SSM_HEADS // SSM_GROUPS
    gw = hpg * SSM_HEADDIM
    bpg = gw // LANES
    for g in range(SSM_GROUPS):
        c_g = cmat_m[:, g * SSM_STATE:(g + 1) * SSM_STATE]
        b_g = b_p[:, g * SSM_STATE:(g + 1) * SSM_STATE]
        cb = lax.dot_general(c_g, b_g, (((1,), (1,)), ((), ())), preferred_element_type=F32)
        h_blk = h_ref[g * gw:(g + 1) * gw, :]
        y_off = lax.dot_general(c_g, h_blk.astype(mm), (((1,), (1,)), ((), ())),
                                preferred_element_type=F32)
        y_off = y_off * ecs_e[:, g * gw:(g + 1) * gw]
        for jb in range(bpg):
            blk = g * bpg + jb
            x_blk = xdt_p[:, blk * LANES:(blk + 1) * LANES]
            acc = None
            for par in range(2):
                h = 2 * blk + par
                diff = cs[:, h:h + 1] - cs_t[h:h + 1, :]
                lm = jnp.exp(jnp.where(causal, diff, -jnp.inf))
                gmat = (cb * lm).astype(mm)
                x_half = jnp.where(lo_half if par == 0 else jnp.logical_not(lo_half), x_blk,
                                   jnp.zeros_like(x_blk))
                t = jnp.dot(gmat, x_half, preferred_element_type=F32)
                acc = t if acc is None else acc + t
            y_blocks.append(acc + y_off[:, jb * LANES:(jb + 1) * LANES])
        xdd_t = jnp.concatenate(
            [xdd_p[:, (g * bpg + jb) * LANES:(g * bpg + jb + 1) * LANES].T for jb in range(bpg)],
            axis=0).astype(mm)
        s_new = jnp.dot(xdd_t, b_g, preferred_element_type=F32)
        for r in range(hpg):
            h = g * hpg + r
            scal = jnp.exp(cs_t[h:h + 1, L - 1:L])
            rows = slice(h * SSM_HEADDIM, (h + 1) * SSM_HEADDIM)
            h_ref[rows, :] = h_ref[rows, :] * scal + s_new[r * SSM_HEADDIM:(r + 1) * SSM_HEADDIM, :]

    y = jnp.concatenate(y_blocks, axis=1) + dsk * xs
    y = y * _silu(z)
    ms = jnp.mean(y * y, axis=-1, keepdims=True)
    return y * lax.rsqrt(ms + 1e-5) * nw


def _ssd_kernel(*refs, L, nc, nb, nsub, zero_state, n_alias):
    xs_ref, bc_ref, z_ref, xn_ref, wdt_ref = refs[:5]
    refs = refs[5:]
    if not zero_state:
        pre_ref, h0_ref = refs[:2]
        refs = refs[2:]
    cw_ref, cb_ref, dtb_ref, alog_ref, dsk_ref, nw_ref, tri_ref, e_ref = refs[:8]
    y_ref, hout_ref, cst_ref, xp_ref, h_ref = refs[8 + n_alias:]
    c = pl.program_id(1)
    off = SUBLANES - (SSM_CONV - 1)

    dtr_all = lax.dot_general(xn_ref[...], wdt_ref[0].astype(BF16), (((1,), (1,)), ((), ())),
                              preferred_element_type=F32)
    stats = _head_stats(dtr_all, dtb_ref[...], alog_ref[...], tri_ref[...], e_ref[...], L=L,
                        nb=nb * nsub)
    xs_all = xs_ref[...].astype(F32)
    bc_all = bc_ref[...].astype(F32)
    z_all = z_ref[...].astype(F32)
    ys = []
    for ci in range(nb * nsub):
        bi, sub = divmod(ci, nsub)
        rows = slice(ci * L, (ci + 1) * L)

        def carry_conv_rows(bi=bi):
            xp_ref[bi, off:SUBLANES, :] = xp_ref[bi, L + off:L + SUBLANES, :]

        if sub == 0:
            @pl.when(c == 0)
            def _():
                if zero_state:
                    xp_ref[bi, off:SUBLANES, :] = jnp.zeros((SSM_CONV - 1, 2 * WIDTH), F32)
                    h_ref[bi] = jnp.zeros((WIDTH, SSM_STATE), F32)
                else:
                    xp_ref[bi, off:SUBLANES, :] = pre_ref[0, bi]
                    h_ref[bi] = h0_ref[0, bi]

            if nc > 1:
                pl.when(c > 0)(carry_conv_rows)
        else:
            carry_conv_rows()

        xp_ref[bi, SUBLANES:SUBLANES + L, 0:WIDTH] = xs_all[rows, :]
        xp_ref[bi, SUBLANES:SUBLANES + L, WIDTH:2 * WIDTH] = bc_all[rows, :]

        conv = cb_ref[...] + cw_ref[0:1, :] * xp_ref[bi, off:off + L, :]
        for k in range(1, SSM_CONV):
            conv = conv + cw_ref[k:k + 1, :] * xp_ref[bi, off + k:off + k + L, :]
        ys.append(_ssd_core(_silu(conv), z_all[rows, :], stats[ci], h_ref.at[bi], dsk_ref[...],
                            nw_ref[...], L=L))

        if sub == nsub - 1:
            @pl.when(c == nc - 1)
            def _():
                hout_ref[0, bi] = h_ref[bi]
                cst_ref[0, bi] = xp_ref[bi, L + off:L + SUBLANES, :]

    y_ref[...] = jnp.concatenate(ys, axis=0).astype(y_ref.dtype)


def ssd_branch(proj, xn, w_t, lp, consts, l, *, nbatch, t, conv_state=None, ssm_state=None, prev=None):
    L = SSM_CHUNK if t % SSM_CHUNK == 0 else t
    nsub = 2 if (t // L) % 2 == 0 else 1
    nc = t // (L * nsub)
    zero_state = conv_state is None
    nb = 8 if t == L and nbatch % 8 == 0 else 1
    m = nbatch * t
    rl = nb * nsub * L

    def rows(b, c):
        return b * nc + c

    def col(cidx):
        return pl.BlockSpec((rl, WIDTH), lambda b, c: (rows(b, c), cidx))

    tri, e = consts
    in_specs = [col(COL_XS), col(COL_BC), col(COL_Z),
                pl.BlockSpec((rl, D_MODEL), lambda b, c: (rows(b, c), 0)),
                pl.BlockSpec((1, LANES, D_MODEL), lambda b, c: (l, _SEG_START["dt"] // LANES, 0))]
    args = [proj, proj, proj, xn, w_t]
    if not zero_state:
        in_specs += [pl.BlockSpec((1, nb, SSM_CONV - 1, 2 * WIDTH), lambda b, c: (l, b, 0, 0)),
                     pl.BlockSpec((1, nb, WIDTH, SSM_STATE), lambda b, c: (l, b, 0, 0))]
        args += [conv_state, ssm_state]
    in_specs += [_const_spec((SSM_CONV, 2 * WIDTH)), _const_spec((1, 2 * WIDTH)),
                 _const_spec((1, LANES)), _const_spec((1, LANES)), _const_spec((1, WIDTH)),
                 _const_spec((1, WIDTH)), _const_spec((SSM_CHUNK, 3 * SSM_CHUNK)),
                 _const_spec((2 * LANES, WIDTH))]
    args += [lp["ssm_conv_w"], lp["ssm_conv_b"], lp["dt_bias"], lp["a_log"], lp["d_skip"],
             lp["ssm_norm_w"], tri, e]
    h_shape, h_spec, h_prev = _stacked_out(l, nbatch, nb, (WIDTH, SSM_STATE), prev and prev[0])
    c_shape, c_spec, c_prev = _stacked_out(l, nbatch, nb, (SSM_CONV - 1, 2 * WIDTH), prev and prev[1])
    alias_in = h_prev + c_prev
    aliases = {len(args) + i: 1 + i for i in range(len(alias_in))}
    in_specs += [pl.BlockSpec(memory_space=pl.ANY)] * len(alias_in)
    y, h_out, c_out = pl.pallas_call(
        functools.partial(_ssd_kernel, L=L, nc=nc, nb=nb, nsub=nsub, zero_state=zero_state,
                          n_alias=len(alias_in)),
        grid=(nbatch // nb, nc),
        in_specs=in_specs,
        out_specs=[pl.BlockSpec((rl, WIDTH), lambda b, c: (rows(b, c), 0)), h_spec, c_spec],
        out_shape=[jax.ShapeDtypeStruct((m, WIDTH), BF16), h_shape, c_shape],
        scratch_shapes=[pltpu.VMEM((nb, SUBLANES + L, 2 * WIDTH), F32),
                        pltpu.VMEM((nb, WIDTH, SSM_STATE), F32)],
        input_output_aliases=aliases,
        compiler_params=_cparams(("parallel", "arbitrary")),
        name="ssd_branch",
    )(*args, *alias_in)
    return y, (h_out, c_out)


def _dwconv(xp_ref, xsh_ref, w_ref, b_ref, acc_ref, *, k, start, tb, row0):
    rbk = min(tb, 32)
    if xsh_ref is not None:
        nrows = xsh_ref.shape[1]
        for s in range(1, SUBLANES):
            xsh_ref[s - 1] = xp_ref[s:s + nrows, :]
    for j in range(WIDTH // LANES):
        lanes = slice(j * LANES, (j + 1) * LANES)
        for r in range(tb // rbk):
            r0 = r * rbk
            acc = None
            for i in range(k):
                first = start + i + r0
                s = first % SUBLANES
                if xsh_ref is None or s == 0:
                    x = xp_ref[first:first + rbk, lanes]
                else:
                    x = xsh_ref[s - 1, first - s:first - s + rbk, lanes]
                term = w_ref[i:i + 1, lanes] * x
                if acc is None:
                    acc = term if b_ref is None else term + b_ref[:, lanes]
                else:
                    acc = acc + term
            acc_ref[row0 + r0:row0 + r0 + rbk, lanes] = acc


def _conv_all(u, xp_ref, pre_ref, st_ref, w_ref, b_ref, acc_ref, xsh_ref=None, *, k, p, tb, nc, nb):
    c = pl.program_id(1)
    for bi in range(nb):
        @pl.when(c == 0)
        def _():
            if pre_ref is None:
                xp_ref[bi, p - (k - 1):p, :] = jnp.zeros((k - 1, WIDTH), F32)
            else:
                xp_ref[bi, p - (k - 1):p, :] = pre_ref[0, bi]

        if nc > 1:
            @pl.when(c > 0)
            def _():
                xp_ref[bi, 0:p, :] = xp_ref[bi, tb:tb + p, :]

        xp_ref[bi, p:p + tb, :] = u[bi * tb:(bi + 1) * tb, :]
        _dwconv(xp_ref.at[bi], xsh_ref, w_ref, b_ref, acc_ref, k=k, start=p - (k - 1), tb=tb,
                row0=bi * tb)

        @pl.when(c == nc - 1)
        def _():
            st_ref[0, bi] = xp_ref[bi, p + tb - (k - 1):p + tb, :]


def _cf_kernel(*refs, tb, nc, nb, p, zero_state, n_alias):
    a_ref, g_ref, gate_ref = refs[:3]
    refs = refs[3:]
    pre_ref = None
    if not zero_state:
        pre_ref, refs = refs[0], refs[1:]
    w_ref, b_ref, lnw_ref, lnb_ref = refs[:4]
    o_ref, st_ref, xp_ref, acc_ref, *xsh = refs[4 + n_alias:]
    u = a_ref[...].astype(F32) * _sigmoid(g_ref[...].astype(F32))
    _conv_all(u, xp_ref, pre_ref, st_ref, w_ref, b_ref, acc_ref, *xsh, k=CF_CONV, p=p, tb=tb, nc=nc,
              nb=nb)
    u = acc_ref[...]
    mu = jnp.mean(u, axis=-1, keepdims=True)
    uc = u - mu
    var = jnp.mean(uc * uc, axis=-1, keepdims=True)
    u = _silu(uc * lax.rsqrt(var + 1e-5) * lnw_ref[...] + lnb_ref[...])
    o_ref[...] = (u * _silu(gate_ref[...].astype(F32))).astype(o_ref.dtype)


def _sc_kernel(*refs, tb, nc, nb, p, zero_state, n_alias):
    gb_ref, gc_ref, sv_ref, gate_ref = refs[:4]
    refs = refs[4:]
    pre_ref = None
    if not zero_state:
        pre_ref, refs = refs[0], refs[1:]
    w_ref = refs[0]
    o_ref, st_ref, xp_ref, acc_ref = refs[1 + n_alias:]
    u = gc_ref[...].astype(F32) * sv_ref[...].astype(F32)
    _conv_all(u, xp_ref, pre_ref, st_ref, w_ref, None, acc_ref, k=SC_CONV, p=p, tb=tb, nc=nc, nb=nb)
    o_ref[...] = (gb_ref[...].astype(F32) * acc_ref[...]
                  * _silu(gate_ref[...].astype(F32))).astype(o_ref.dtype)


def conv_branch(kind, proj, lp, l, *, nbatch, t, state=None, prev=None):
    tb = min(t, 512)
    nc = t // tb
    nb = 8 if nc == 1 and nbatch % 8 == 0 else 1
    m = nbatch * t
    rl = nb * tb
    zero_state = state is None
    k = CF_CONV if kind == "cf" else SC_CONV
    p = -(-(k - 1) // SUBLANES) * SUBLANES

    def rows(b, c):
        return b * nc + c

    def col(cidx):
        return pl.BlockSpec((rl, WIDTH), lambda b, c: (rows(b, c), cidx))

    if kind == "cf":
        in_specs = [col(COL_CF_A), col(COL_CF_G), col(COL_CF_GATE)]
        args = [proj, proj, proj]
        w_specs = [_const_spec((k, WIDTH)), _const_spec((1, WIDTH)), _const_spec((1, WIDTH)),
                   _const_spec((1, WIDTH))]
        w_args = [lp["cf_conv_w"], lp["cf_conv_b"], lp["cf_ln_w"], lp["cf_ln_b"]]
        body = _cf_kernel
    else:
        in_specs = [col(COL_SC_B), col(COL_SC_C), col(COL_SC_V), col(COL_SC_GATE)]
        args = [proj, proj, proj, proj]
        w_specs = [_const_spec((k, WIDTH))]
        w_args = [lp["sc_conv_w"]]
        body = _sc_kernel
    if not zero_state:
        in_specs.append(pl.BlockSpec((1, nb, k - 1, WIDTH), lambda b, c: (l, b, 0, 0)))
        args.append(state)
    in_specs += w_specs
    args += w_args
    s_shape, s_spec, alias_in = _stacked_out(l, nbatch, nb, (k - 1, WIDTH), prev)
    aliases = {len(args) + i: 1 + i for i in range(len(alias_in))}
    in_specs += [pl.BlockSpec(memory_space=pl.ANY)] * len(alias_in)
    scratch = [pltpu.VMEM((nb, p + tb, WIDTH), F32), pltpu.VMEM((rl, WIDTH), F32)]
    if kind == "cf" and nb == 1 and tb % 32 == 0:
        scratch.append(pltpu.VMEM((SUBLANES - 1, p + tb - SUBLANES, WIDTH), F32))
    return pl.pallas_call(
        functools.partial(body, tb=tb, nc=nc, nb=nb, p=p, zero_state=zero_state, n_alias=len(alias_in)),
        grid=(nbatch // nb, nc),
        in_specs=in_specs,
        out_specs=[pl.BlockSpec((rl, WIDTH), lambda b, c: (rows(b, c), 0)), s_spec],
        out_shape=[jax.ShapeDtypeStruct((m, WIDTH), BF16), s_shape],
        scratch_shapes=scratch,
        input_output_aliases=aliases,
        compiler_params=_cparams(("parallel", "arbitrary")),
        name=kind + "_branch",
    )(*args, *alias_in)


def _sink_column(sink_ref, g, rows_per_head):
    n = ATT_REP * rows_per_head
    rid = lax.broadcasted_iota(jnp.int32, (n, 1), 0)
    col = jnp.full((n, 1), sink_ref[g * ATT_REP + ATT_REP - 1], F32)
    for r in range(ATT_REP - 2, -1, -1):
        col = jnp.where(rid < (r + 1) * rows_per_head, sink_ref[g * ATT_REP + r], col)
    return col


def _softmax_sink(s, ok, sink_col):
    s = jnp.where(ok, s, -jnp.inf)
    m = jnp.maximum(jnp.max(s, axis=-1, keepdims=True), sink_col)
    p = jnp.exp(s - m)
    denom = jnp.sum(p, axis=-1, keepdims=True) + jnp.exp(sink_col - m)
    return p / denom


def _attn_prompt_kernel(q_ref, kp_ref, kc_ref, vp_ref, vc_ref, gate_ref, sink_ref, *rest, nblk, n_alias):
    o_ref, kw_ref, vw_ref = rest[n_alias:]
    L = WINDOW
    i = pl.program_id(1)
    q = q_ref[...].astype(F32) * (ATT_HEAD_DIM ** -0.5)
    k2 = jnp.concatenate([kp_ref[...], kc_ref[...]], axis=0).astype(F32)
    v2 = jnp.concatenate([vp_ref[...], vc_ref[...]], axis=0).astype(F32)
    li = lax.broadcasted_iota(jnp.int32, (ATT_REP * L, 2 * L), 0) & (L - 1)
    sj = lax.broadcasted_iota(jnp.int32, (ATT_REP * L, 2 * L), 1)
    ok = (sj > li) & (sj <= li + WINDOW) & ((sj >= L) | (i > 0))
    lane = lax.broadcasted_iota(jnp.int32, (2 * L, LANES), 1)
    lo_half = lane < ATT_HEAD_DIM

    o_blocks = [None] * (WIDTH // LANES)
    for g in range(ATT_KV_HEADS):
        half = g % 2
        sl = slice((g // 2) * LANES, (g // 2 + 1) * LANES)
        keep = lo_half if half == 0 else jnp.logical_not(lo_half)
        k_use = jnp.where(keep, k2[:, sl], 0.0).astype(BF16)
        v_use = jnp.where(keep, v2[:, sl], 0.0).astype(BF16)
        parts = []
        for r in range(ATT_REP):
            h = g * ATT_REP + r
            qb = q[:, (h // 2) * LANES:(h // 2 + 1) * LANES]
            parts.append(qb if h % 2 == half else pltpu.roll(qb, ATT_HEAD_DIM, 1))
        lhs = jnp.concatenate(parts, axis=0).astype(BF16)
        s = lax.dot_general(lhs, k_use, (((1,), (1,)), ((), ())), preferred_element_type=F32)
        pn = _softmax_sink(s, ok, _sink_column(sink_ref, g, L)).astype(BF16)
        o = jnp.dot(pn, v_use, preferred_element_type=F32)
        for r in range(ATT_REP):
            h = g * ATT_REP + r
            ob = o[r * L:(r + 1) * L, :]
            if h % 2 != half:
                ob = pltpu.roll(ob, ATT_HEAD_DIM, 1)
            o_blocks[h // 2] = ob if o_blocks[h // 2] is None else o_blocks[h // 2] + ob
    o = jnp.concatenate(o_blocks, axis=1)
    o_ref[...] = (o * _silu(gate_ref[...].astype(F32))).astype(o_ref.dtype)

    @pl.when(i == nblk - 1)
    def _():
        kw_ref[0, 0] = kc_ref[...].astype(F32).T.reshape(ATT_KV_HEADS, ATT_HEAD_DIM, WINDOW)
        vw_ref[0, 0] = vc_ref[...].astype(F32).T.reshape(ATT_KV_HEADS, ATT_HEAD_DIM, WINDOW)


def _win_out(l, nbatch, nb, prev):
    return _stacked_out(l, nbatch, nb, (ATT_KV_HEADS, ATT_HEAD_DIM, WINDOW), prev)


def attn_prompt(proj, sinks, l, *, nbatch, t, prev=None):
    L = WINDOW
    nblk = t // L

    def col(cidx):
        return pl.BlockSpec((L, WIDTH), lambda b, i: (b * nblk + i, cidx))

    def kv(cidx, back):
        return pl.BlockSpec((L, ATT_KV_WIDTH), lambda b, i: (b * nblk + jnp.maximum(i - back, 0), cidx))

    k_shape, k_spec, k_prev = _win_out(l, nbatch, 1, prev and prev[0])
    v_shape, v_spec, v_prev = _win_out(l, nbatch, 1, prev and prev[1])
    alias_in = k_prev + v_prev
    args = [proj, proj, proj, proj, proj, proj, sinks]
    aliases = {len(args) + i: 1 + i for i in range(len(alias_in))}
    y, kw, vw = pl.pallas_call(
        functools.partial(_attn_prompt_kernel, nblk=nblk, n_alias=len(alias_in)),
        grid=(nbatch, nblk),
        in_specs=[col(COL_Q), kv(COL_K, 1), kv(COL_K, 0), kv(COL_V, 1), kv(COL_V, 0),
                  col(COL_ATT_GATE), pl.BlockSpec(memory_space=pltpu.SMEM)]
        + [pl.BlockSpec(memory_space=pl.ANY)] * len(alias_in),
        out_specs=[pl.BlockSpec((L, WIDTH), lambda b, i: (b * nblk + i, 0)), k_spec, v_spec],
        out_shape=[jax.ShapeDtypeStruct((nbatch * t, WIDTH), BF16), k_shape, v_shape],
        input_output_aliases=aliases,
        compiler_params=_cparams(("parallel", "arbitrary")),
        name="attn_prompt",
    )(*args, *alias_in)
    return y, (kw, vw)


def _attn_sample_kernel(q_ref, kn_ref, vn_ref, ck_ref, cv_ref, gate_ref, sink_ref, *rest, L, nb, n_alias):
    o_ref, kw_ref, vw_ref = rest[n_alias:]
    q = q_ref[...].astype(F32) * (ATT_HEAD_DIM ** -0.5)
    kn_t = _pad_rows(kn_ref[...].astype(F32), WINDOW).T
    vn_t = _pad_rows(vn_ref[...].astype(F32), WINDOW).T
    nkeys = 2 * WINDOW
    rq = ATT_REP * L
    row = lax.broadcasted_iota(jnp.int32, (nb * rq, nkeys), 0)
    sj = lax.broadcasted_iota(jnp.int32, (nb * rq, nkeys), 1)
    li = row & (L - 1)
    cj = sj - WINDOW - (row // rq) * L
    ok = ((sj < WINDOW) & (sj > li)) | ((cj >= 0) & (cj <= li))
    head_of_row = (lax.broadcasted_iota(jnp.int32, (nb * rq, 1), 0) // L) % ATT_REP
    lane = lax.broadcasted_iota(jnp.int32, (ATT_HEAD_DIM, WINDOW), 1)
    is_new = lane >= WINDOW - L
    pieces = [[None] * ATT_HEADS for _ in range(nb)]
    for g in range(ATT_KV_HEADS):
        gl = slice(g * ATT_HEAD_DIM, (g + 1) * ATT_HEAD_DIM)
        kn_g, vn_g = kn_t[gl, :], vn_t[gl, :]
        scores, values = [], []
        for bi in range(nb):
            rows = slice(bi * L, (bi + 1) * L)
            kc, vc = ck_ref[0, bi, g], cv_ref[0, bi, g]
            k2 = jnp.concatenate([kc, kn_g], axis=1).astype(BF16)
            values.append(jnp.concatenate([vc, vn_g], axis=1).astype(BF16))
            lhs = jnp.concatenate(
                [q[rows, (g * ATT_REP + r) * ATT_HEAD_DIM:(g * ATT_REP + r + 1) * ATT_HEAD_DIM]
                 for r in range(ATT_REP)], axis=0).astype(BF16)
            scores.append(jnp.dot(lhs, k2, preferred_element_type=F32))
            shift_new = (WINDOW - L - bi * L) % WINDOW
            for c_old, n_g, w_ref in ((kc, kn_g, kw_ref), (vc, vn_g, vw_ref)):
                tail = n_g if shift_new == 0 else pltpu.roll(n_g, shift_new, 1)
                w_ref[0, bi, g] = jnp.where(is_new, tail, pltpu.roll(c_old, WINDOW - L, 1))
        sink_col = jnp.full((nb * rq, 1), sink_ref[g * ATT_REP], F32)
        for r in range(1, ATT_REP):
            sink_col = jnp.where(head_of_row == r, sink_ref[g * ATT_REP + r], sink_col)
        pn = _softmax_sink(jnp.concatenate(scores, axis=0), ok, sink_col).astype(BF16)
        for bi in range(nb):
            o = lax.dot_general(pn[bi * rq:(bi + 1) * rq, :], values[bi], (((1,), (1,)), ((), ())),
                                preferred_element_type=F32)
            for r in range(ATT_REP):
                pieces[bi][g * ATT_REP + r] = o[r * L:(r + 1) * L, :]
    o = jnp.concatenate([jnp.concatenate(p, axis=1) for p in pieces], axis=0)
    o_ref[...] = (o * _silu(gate_ref[...].astype(F32))).astype(o_ref.dtype)


def attn_sample(proj, sinks, cache_k, cache_v, l, *, nbatch, t, prev=None):
    nb = WINDOW // t if nbatch % (WINDOW // t) == 0 else nbatch
    rl = nb * t

    def col(cidx, w):
        return pl.BlockSpec((rl, w), lambda b: (b, cidx))

    cache = pl.BlockSpec((1, nb, ATT_KV_HEADS, ATT_HEAD_DIM, WINDOW), lambda b: (l, b, 0, 0, 0))
    k_shape, k_spec, k_prev = _win_out(l, nbatch, nb, prev and prev[0])
    v_shape, v_spec, v_prev = _win_out(l, nbatch, nb, prev and prev[1])
    alias_in = k_prev + v_prev
    args = [proj, proj, proj, cache_k, cache_v, proj, sinks]
    aliases = {len(args) + i: 1 + i for i in range(len(alias_in))}
    y, kw, vw = pl.pallas_call(
        functools.partial(_attn_sample_kernel, L=t, nb=nb, n_alias=len(alias_in)),
        grid=(nbatch // nb,),
        in_specs=[col(COL_Q, WIDTH), col(COL_K, ATT_KV_WIDTH), col(COL_V, ATT_KV_WIDTH), cache, cache,
                  col(COL_ATT_GATE, WIDTH), pl.BlockSpec(memory_space=pltpu.SMEM)]
        + [pl.BlockSpec(memory_space=pl.ANY)] * len(alias_in),
        out_specs=[pl.BlockSpec((rl, WIDTH), lambda b: (b, 0)), k_spec, v_spec],
        out_shape=[jax.ShapeDtypeStruct((nbatch * t, WIDTH), BF16), k_shape, v_shape],
        input_output_aliases=aliases,
        compiler_params=_cparams(("parallel",)),
        name="attn_sample",
    )(*args, *alias_in)
    return y, (kw, vw)


def _prep_layer(l, norm_w, w_in, ssm_conv_w, ssm_conv_b, ssm_dt_bias, ssm_a_log, ssm_d, ssm_norm_w,
                w_out_ssm, cf_conv_w, cf_conv_b, cf_ln_w, cf_ln_b, w_out_cf, sc_conv_w, w_out_sc,
                att_sinks, w_out_att, w_o):
    del w_in
    pad_h = (0, LANES - SSM_HEADS)
    return dict(
        norm_w=norm_w[l],
        ssm_conv_w=ssm_conv_w[l], ssm_conv_b=ssm_conv_b[l].reshape(1, -1),
        dt_bias=jnp.pad(ssm_dt_bias[l], pad_h).reshape(1, LANES),
        a_log=jnp.pad(ssm_a_log[l], pad_h).reshape(1, LANES),
        d_skip=jnp.repeat(ssm_d[l], SSM_HEADDIM).reshape(1, WIDTH),
        ssm_norm_w=ssm_norm_w[l].reshape(1, WIDTH),
        cf_conv_w=cf_conv_w[l], cf_conv_b=cf_conv_b[l].reshape(1, WIDTH),
        cf_ln_w=cf_ln_w[l].reshape(1, WIDTH), cf_ln_b=cf_ln_b[l].reshape(1, WIDTH),
        sc_conv_w=sc_conv_w[l], sinks=att_sinks[l],
        w_outs=tuple(w.astype(BF16) for w in (w_out_ssm, w_out_cf, w_out_sc, w_out_att)),
        w_o=w_o.astype(BF16),
    )


def _layer(x, xn, w_t, lp, consts, l, states, prev, next_norm_w, *, nbatch, t):
    kw = dict(nbatch=nbatch, t=t)
    proj = in_proj(xn, w_t, l, tm=2048)
    if states is None:
        ya, st_a = ssd_branch(proj, xn, w_t, lp, consts, l, prev=prev and prev[0], **kw)
        yb, st_b = conv_branch("cf", proj, lp, l, prev=prev and prev[1], **kw)
        yc, st_c = conv_branch("sc", proj, lp, l, prev=prev and prev[2], **kw)
        yd, st_d = attn_prompt(proj, lp["sinks"], l, prev=prev and prev[3], **kw)
    else:
        ssm, conv_ssm, conv_cf, conv_sc, cache_k, cache_v = states
        ya, st_a = ssd_branch(proj, xn, w_t, lp, consts, l, conv_state=conv_ssm, ssm_state=ssm,
                              prev=prev and prev[0], **kw)
        yb, st_b = conv_branch("cf", proj, lp, l, state=conv_cf, prev=prev and prev[1], **kw)
        yc, st_c = conv_branch("sc", proj, lp, l, state=conv_sc, prev=prev and prev[2], **kw)
        yd, st_d = attn_sample(proj, lp["sinks"], cache_k, cache_v, l, prev=prev and prev[3], **kw)
    outs = merge_out_norm((ya, yb, yc, yd), lp["w_outs"], proj, lp["w_o"], x, next_norm_w, l, EPS,
                          last=l == DEPTH - 1, tm=256)
    return outs, (st_a, st_b, st_c, st_d)


def kernel(x_prompt, x_sample, state_ssm, state_conv_ssm, state_conv_cf, state_conv_sc, cache_k, cache_v, norm_w, w_in, ssm_conv_w, ssm_conv_b, ssm_dt_bias, ssm_a_log, ssm_d, ssm_norm_w, w_out_ssm, cf_conv_w, cf_conv_b, cf_ln_w, cf_ln_b, w_out_cf, sc_conv_w, w_out_sc, att_sinks, w_out_att, w_o, final_norm_w):
    bp, tp, _ = x_prompt.shape
    bs, ts, _ = x_sample.shape
    xp = x_prompt.reshape(bp * tp, D_MODEL)
    xs = x_sample.reshape(bs * ts, D_MODEL)

    tri = jnp.tril(jnp.ones((SSM_CHUNK, SSM_CHUNK), BF16))
    expand = (jnp.arange(WIDTH)[None, :] // SSM_HEADDIM == jnp.arange(LANES)[:, None]).astype(BF16)
    consts = (jnp.tile(tri, (1, 3)), jnp.tile(expand, (2, 1)))
    to_win_minor = lambda c: jnp.transpose(c, (0, 1, 3, 4, 2))
    from_win_minor = lambda c: jnp.transpose(c, (0, 1, 4, 2, 3))
    w_t = jnp.swapaxes(w_in, 1, 2)
    s_states = (state_ssm.reshape(DEPTH, bs, WIDTH, SSM_STATE), state_conv_ssm, state_conv_cf,
                state_conv_sc, to_win_minor(cache_k), to_win_minor(cache_v))

    xnp = rmsnorm(xp, norm_w[0], EPS, BF16)
    xns = rmsnorm(xs, norm_w[0], EPS, BF16)
    p_prev = s_prev = None
    for l in range(DEPTH):
        lp = _prep_layer(l, norm_w, w_in, ssm_conv_w, ssm_conv_b, ssm_dt_bias, ssm_a_log, ssm_d,
                         ssm_norm_w, w_out_ssm, cf_conv_w, cf_conv_b, cf_ln_w, cf_ln_b, w_out_cf,
                         sc_conv_w, w_out_sc, att_sinks, w_out_att, w_o)
        next_w = norm_w[l + 1] if l + 1 < DEPTH else final_norm_w
        outp, p_prev = _layer(xp, xnp, w_t, lp, consts, l, None, p_prev, next_w, nbatch=bp, t=tp)
        outs, s_prev = _layer(xs, xns, w_t, lp, consts, l, s_states, s_prev, next_w, nbatch=bs, t=ts)
        if l + 1 < DEPTH:
            (xp, xnp), (xs, xns) = outp, outs
    y_prompt = outp[0].reshape(bp, tp, D_MODEL)
    y_sample = outs[0].reshape(bs, ts, D_MODEL)

    def unpack(st, nbatch):
        (h, conv_ssm), conv_cf, conv_sc, (kwin, vwin) = st
        return (h.reshape(DEPTH, nbatch, SSM_HEADS, SSM_HEADDIM, SSM_STATE), conv_ssm, conv_cf,
                conv_sc, from_win_minor(kwin), from_win_minor(vwin))

    return (y_prompt, y_sample, *unpack(p_prev, bp), *unpack(s_prev, bs))
```
